```python
import math
import jax, jax.numpy as jnp
from jax import lax
import numpy as np

D_MODEL = 1024
BATCH = 16
SEQ = 2048
DEPTH = 1
DEC_BATCH = 16
DEC_SEQ = 4096
PAST_LEN = 128

GRID_W = 64
N_Q_HEADS = 8
N_KV_HEADS = 2
HEAD_DIM = 64
ATTN_W = N_Q_HEADS * HEAD_DIM
KV_W = N_KV_HEADS * HEAD_DIM
ROPE_THETA = 10000.0
Q_BLOCK = 128
SSD_HEADS = 8
SSD_HEAD_DIM = 64
D_INNER = SSD_HEADS * SSD_HEAD_DIM
SSD_GROUPS = 2
D_STATE = 128
D_CONV = 5
CHUNK = 128
CONV_CH = D_INNER + 2 * SSD_GROUPS * D_STATE
MIX_W = ATTN_W + D_INNER
_K0 = ATTN_W
_V0 = _K0 + KV_W
_Z0 = _V0 + KV_W
_XBC0 = _Z0 + D_INNER
_DT0 = _XBC0 + CONV_CH
IN_COLS = _DT0 + 2 * SSD_HEADS
N_EXPERTS = 16
CAPACITY_FACTOR = 2
D_EXPERT = 1024
PLE_DIM = 256
DN_ALPHA = (2.0 * DEPTH) ** 0.25
DN_BETA = (8.0 * DEPTH) ** -0.25
NORM_EPS = 1e-6
LN_EPS = 1e-5

kernel_name = "hymba_gqa_ssd_expert_choice_encoder"


def _rms_norm(x, w):
    xf = x.astype(jnp.float32)
    y = xf * lax.rsqrt(jnp.mean(xf * xf, axis=-1, keepdims=True) + NORM_EPS)
    return (y * w.astype(jnp.float32)).astype(x.dtype)


def _layer_norm(x, g, b):
    xf = x.astype(jnp.float32)
    xc = xf - jnp.mean(xf, axis=-1, keepdims=True)
    y = xc * lax.rsqrt(jnp.mean(xc * xc, axis=-1, keepdims=True) + LN_EPS)
    return (y * g.astype(jnp.float32) + b.astype(jnp.float32)).astype(x.dtype)


def _axial_rope_tables(seq_len):
    rows = seq_len // GRID_W
    row_id = jnp.broadcast_to(jnp.arange(rows)[:, None], (rows, GRID_W)).reshape(seq_len)
    col_id = jnp.broadcast_to(jnp.arange(GRID_W)[None, :], (rows, GRID_W)).reshape(seq_len)
    pos = jnp.stack([row_id, col_id], axis=-1).astype(jnp.float32)
    n_freq = HEAD_DIM // 4
    inv_freq = ROPE_THETA ** (-jnp.arange(n_freq, dtype=jnp.float32) / n_freq)
    ang = pos[:, :, None] * inv_freq
    return jnp.cos(ang), jnp.sin(ang)


def _apply_axial_rope(x, cos, sin):
    b, s, h, d = x.shape
    xr = x.astype(jnp.float32).reshape(b, s, h, 2, 2, d // 4)
    x1, x2 = xr[..., 0, :], xr[..., 1, :]
    c = cos[None, :, None]
    sn = sin[None, :, None]
    out = jnp.stack([x1 * c - x2 * sn, x1 * sn + x2 * c], axis=-2)
    return out.reshape(b, s, h, d).astype(x.dtype)


def _attention(q, k, v):
    b, s, _, hd = q.shape
    grp = N_Q_HEADS // N_KV_HEADS
    nb = s // Q_BLOCK
    qb = (q * (hd ** -0.5)).reshape(b, nb, Q_BLOCK, N_KV_HEADS, grp, hd).transpose(1, 0, 2, 3, 4, 5)

    def attend_block(q_blk):
        sc = jnp.einsum("bqkgd,bskd->bkgqs", q_blk, k).astype(jnp.float32)
        pr = jax.nn.softmax(sc, axis=-1).astype(v.dtype)
        return jnp.einsum("bkgqs,bskd->bqkgd", pr, v)

    o = lax.map(attend_block, qb)
    return o.transpose(1, 0, 2, 3, 4, 5).reshape(b, s, N_Q_HEADS * hd)


def _dwconv_centred(u, w, bias):
    y = lax.conv_general_dilated(
        u, w[:, None, :].astype(u.dtype), window_strides=(1,),
        padding=((D_CONV // 2, D_CONV // 2),),
        dimension_numbers=("NWC", "WIO", "NWC"),
        feature_group_count=u.shape[-1])
    return y + bias.astype(u.dtype)


def _ssd_chunked(xdt, a, bm, cm):
    b, L, H, P = xdt.shape
    G, N = bm.shape[2], bm.shape[3]
    R = H // G
    c = L // CHUNK
    x = xdt.reshape(b, c, CHUNK, G, R, P)
    a = a.reshape(b, c, CHUNK, G, R).transpose(0, 3, 4, 1, 2)
    bc = bm.reshape(b, c, CHUNK, G, N)
    cc = cm.reshape(b, c, CHUNK, G, N)
    a_cum = jnp.cumsum(a, axis=-1)
    lower = jnp.tril(jnp.ones((CHUNK, CHUNK), dtype=bool))
    seg = a_cum[..., :, None] - a_cum[..., None, :]
    decay_in = jnp.exp(jnp.where(lower, seg, -jnp.inf))
    cb = jnp.einsum("bclgn,bcsgn->bgcls", cc, bc)
    y_diag = jnp.einsum("bgcls,bgrcls,bcsgrp->bclgrp", cb, decay_in, x)
    decay_states = jnp.exp(a_cum[..., -1:] - a_cum)
    states = jnp.einsum("bclgn,bgrcl,bclgrp->bcgrpn", bc, decay_states, x)
    chunk_decay = jnp.exp(a_cum[..., -1])

    def step(h, inp):
        st, dec = inp
        return h * dec[..., None, None] + st, h

    h0 = jnp.zeros((b, G, R, P, N), dtype=x.dtype)
    _, prev = lax.scan(step, h0, (states.transpose(1, 0, 2, 3, 4, 5), chunk_decay.transpose(3, 0, 1, 2)))
    y_off = jnp.einsum("bclgn,cbgrpn,bgrcl->bclgrp", cc, prev, jnp.exp(a_cum))
    return (y_diag + y_off).reshape(b, L, H, P)


def _flip_seq(t):
    return jnp.flip(t, axis=1)


def _ssd_mixer(z, xbc, dt_raw, conv_w, conv_b, dt_bias, a_log, d_skip, norm_w):
    b, s, _ = z.shape
    xbc = jax.nn.silu(_dwconv_centred(xbc, conv_w, conv_b)).astype(jnp.float32)
    xs = xbc[..., :D_INNER].reshape(b, s, SSD_HEADS, SSD_HEAD_DIM)
    bm = xbc[..., D_INNER:D_INNER + SSD_GROUPS * D_STATE].reshape(b, s, SSD_GROUPS, D_STATE)
    cm = xbc[..., D_INNER + SSD_GROUPS * D_STATE:].reshape(b, s, SSD_GROUPS, D_STATE)
    dt = jax.nn.softplus(dt_raw.astype(jnp.float32).reshape(b, s, 2, SSD_HEADS) + dt_bias.astype(jnp.float32))
    a = dt * (-jnp.exp(a_log.astype(jnp.float32)))
    y_fwd = _ssd_chunked(xs * dt[:, :, 0, :, None], a[:, :, 0], bm, cm)
    y_bwd = _flip_seq(_ssd_chunked(_flip_seq(xs * dt[:, :, 1, :, None]), _flip_seq(a[:, :, 1]),
                                   _flip_seq(bm), _flip_seq(cm)))
    y = y_fwd + y_bwd + d_skip.astype(jnp.float32)[:, None] * xs
    y = y.reshape(b, s, D_INNER) * jax.nn.silu(z.astype(jnp.float32))
    y = y.reshape(b, s, SSD_GROUPS, D_INNER // SSD_GROUPS)
    y = y * lax.rsqrt(jnp.mean(y * y, axis=-1, keepdims=True) + NORM_EPS)
    y = y.reshape(b, s, D_INNER) * norm_w.astype(jnp.float32)
    return y.astype(z.dtype)


def _expert_choice(h, w_router, w_gate, w_up, w_down):
    n_tok, d = h.shape
    cap = CAPACITY_FACTOR * n_tok // N_EXPERTS
    aff = jax.nn.softmax((h @ w_router).astype(jnp.float32), axis=-1)
    gate, idx = lax.top_k(aff.T, cap)
    xe = h[idx]
    hid = jax.nn.silu(jnp.einsum("ecd,edf->ecf", xe, w_gate)) * jnp.einsum("ecd,edf->ecf", xe, w_up)
    ye = jnp.einsum("ecf,efd->ecd", hid, w_down) * gate[..., None].astype(h.dtype)
    return jnp.zeros_like(h).at[idx.reshape(-1)].add(ye.reshape(-1, d))


def _layer(x, p, w_in, q_norm_w, k_norm_w, conv_w, conv_b, dt_bias, a_log, d_skip, ssd_norm_w,
           w_out, ln1_g, ln1_b, w_router, w_e_gate, w_e_up, w_e_down, w_ple_gate, b_ple_gate,
           w_ple, ln2_g, ln2_b):
    b, s, d = x.shape
    proj = x @ w_in
    q = proj[..., :_K0].reshape(b, s, N_Q_HEADS, HEAD_DIM)
    k = proj[..., _K0:_V0].reshape(b, s, N_KV_HEADS, HEAD_DIM)
    v = proj[..., _V0:_Z0].reshape(b, s, N_KV_HEADS, HEAD_DIM)
    z = proj[..., _Z0:_XBC0]
    xbc = proj[..., _XBC0:_DT0]
    dt_raw = proj[..., _DT0:]
    cos, sin = _axial_rope_tables(s)
    q = _apply_axial_rope(_rms_norm(q, q_norm_w), cos, sin)
    k = _apply_axial_rope(_rms_norm(k, k_norm_w), cos, sin)
    attn_out = _attention(q, k, v)
    ssd_out = _ssd_mixer(z, xbc, dt_raw, conv_w, conv_b, dt_bias, a_log, d_skip, ssd_norm_w)
    mix = jnp.concatenate([attn_out, ssd_out], axis=-1) @ w_out
    x = _layer_norm(DN_ALPHA * x + mix, ln1_g, ln1_b)
    moe = _expert_choice(x.reshape(b * s, d), w_router, w_e_gate, w_e_up, w_e_down).reshape(b, s, d)
    ple = jax.nn.sigmoid(x @ w_ple_gate + b_ple_gate) * (p @ w_ple)
    return _layer_norm(DN_ALPHA * x + moe + ple, ln2_g, ln2_b)


def _trunk(x, p, ln_in_g, ln_in_b, w_in, q_norm_w, k_norm_w, conv_w, conv_b, dt_bias, a_log,
           d_skip, ssd_norm_w, w_out, ln1_g, ln1_b, w_router, w_e_gate, w_e_up, w_e_down,
           w_ple_gate, b_ple_gate, w_ple, ln2_g, ln2_b):
    x = _layer_norm(x, ln_in_g, ln_in_b)
    for l in range(DEPTH):
        x = _layer(x, p[l], w_in[l], q_norm_w[l], k_norm_w[l], conv_w[l], conv_b[l], dt_bias[l],
                   a_log[l], d_skip[l], ssd_norm_w[l], w_out[l], ln1_g[l], ln1_b[l], w_router[l],
                   w_e_gate[l], w_e_up[l], w_e_down[l], w_ple_gate[l], b_ple_gate[l], w_ple[l],
                   ln2_g[l], ln2_b[l])
    return x


def setup_inputs(seed: int = 0) -> dict:
    key = jax.random.key(seed)
    ks = jax.random.split(key, 32)
    f32 = jnp.float32

    def nrm(k, shape, scale):
        return jax.random.normal(k, shape, f32) * scale

    def gain(k, shape):
        return 1.0 + 0.02 * jax.random.normal(k, shape, f32)

    u = jax.random.uniform(ks[12], (DEPTH, 2, SSD_HEADS), f32)
    dt0 = jnp.exp(u * (math.log(0.1) - math.log(0.001)) + math.log(0.001))
    dt_bias = dt0 + jnp.log(-jnp.expm1(-dt0))
    a_log = jnp.log(jax.random.uniform(ks[13], (DEPTH, 2, SSD_HEADS), f32, 1.0, 16.0))
    return {
        "x_prompt": jax.random.normal(ks[0], (BATCH, SEQ, D_MODEL), f32),
        "x_sample": jax.random.normal(ks[1], (DEC_BATCH, DEC_SEQ, D_MODEL), f32),
        "p_prompt": jax.random.normal(ks[2], (DEPTH, BATCH, SEQ, PLE_DIM), f32),
        "p_sample": jax.random.normal(ks[3], (DEPTH, DEC_BATCH, DEC_SEQ, PLE_DIM), f32),
        "ln_in_g": gain(ks[4], (D_MODEL,)),
        "ln_in_b": nrm(ks[5], (D_MODEL,), 0.02),
        "w_in": nrm(ks[6], (DEPTH, D_MODEL, IN_COLS), D_MODEL ** -0.5),
        "q_norm_w": gain(ks[7], (DEPTH, HEAD_DIM)),
        "k_norm_w": gain(ks[8], (DEPTH, HEAD_DIM)),
        "conv_w": nrm(ks[9], (DEPTH, D_CONV, CONV_CH), D_CONV ** -0.5),
        "conv_b": nrm(ks[10], (DEPTH, CONV_CH), 0.02),
        "dt_bias": dt_bias,
        "a_log": a_log,
        "d_skip": gain(ks[11], (DEPTH, SSD_HEADS)),
        "ssd_norm_w": gain(ks[14], (DEPTH, D_INNER)),
        "w_out": nrm(ks[15], (DEPTH, MIX_W, D_MODEL), DN_BETA * MIX_W ** -0.5),
        "ln1_g": gain(ks[16], (DEPTH, D_MODEL)),
        "ln1_b": nrm(ks[17], (DEPTH, D_MODEL), 0.02),
        "w_router": nrm(ks[18], (DEPTH, D_MODEL, N_EXPERTS), D_MODEL ** -0.5),
        "w_e_gate": nrm(ks[19], (DEPTH, N_EXPERTS, D_MODEL, D_EXPERT), D_MODEL ** -0.5),
        "w_e_up": nrm(ks[20], (DEPTH, N_EXPERTS, D_MODEL, D_EXPERT), D_MODEL ** -0.5),
        "w_e_down": nrm(ks[21], (DEPTH, N_EXPERTS, D_EXPERT, D_MODEL), DN_BETA * D_EXPERT ** -0.5),
        "w_ple_gate": nrm(ks[22], (DEPTH, D_MODEL, D_MODEL), D_MODEL ** -0.5),
        "b_ple_gate": nrm(ks[23], (DEPTH, D_MODEL), 0.02),
        "w_ple": nrm(ks[24], (DEPTH, PLE_DIM, D_MODEL), DN_BETA * PLE_DIM ** -0.5),
        "ln2_g": gain(ks[25], (DEPTH, D_MODEL)),
        "ln2_b": nrm(ks[26], (DEPTH, D_MODEL), 0.02),
    }


def reference(x_prompt, x_sample, p_prompt, p_sample, ln_in_g, ln_in_b, w_in, q_norm_w, k_norm_w,
              conv_w, conv_b, dt_bias, a_log, d_skip, ssd_norm_w, w_out, ln1_g, ln1_b, w_router,
              w_e_gate, w_e_up, w_e_down, w_ple_gate, b_ple_gate, w_ple, ln2_g, ln2_b):
    y_prompt = _trunk(x_prompt, p_prompt, ln_in_g, ln_in_b, w_in, q_norm_w, k_norm_w, conv_w, conv_b,
                      dt_bias, a_log, d_skip, ssd_norm_w, w_out, ln1_g, ln1_b, w_router, w_e_gate,
                      w_e_up, w_e_down, w_ple_gate, b_ple_gate, w_ple, ln2_g, ln2_b)
    y_sample = _trunk(x_sample, p_sample, ln_in_g, ln_in_b, w_in, q_norm_w, k_norm_w, conv_w, conv_b,
                      dt_bias, a_log, d_skip, ssd_norm_w, w_out, ln1_g, ln1_b, w_router, w_e_gate,
                      w_e_up, w_e_down, w_ple_gate, b_ple_gate, w_ple, ln2_g, ln2_b)
    return (y_prompt, y_sample)
```

```python
import functools
import math

import jax
import jax.numpy as jnp
from jax import lax
from jax.experimental import pallas as pl
from jax.experimental.pallas import tpu as pltpu

F32 = jnp.float32
BF16 = jnp.bfloat16
I32 = jnp.int32

D_MODEL = 1024
DEPTH = 1
GRID_W = 64
N_Q_HEADS = 8
N_KV_HEADS = 2
HEAD_DIM = 64
Q_PER_KV = N_Q_HEADS // N_KV_HEADS
ATTN_W = N_Q_HEADS * HEAD_DIM
KV_W = N_KV_HEADS * HEAD_DIM
ROPE_THETA = 10000.0
SSD_HEADS = 8
SSD_HEAD_DIM = 64
D_INNER = SSD_HEADS * SSD_HEAD_DIM
SSD_GROUPS = 2
HEADS_PER_GROUP = SSD_HEADS // SSD_GROUPS
D_STATE = 128
D_CONV = 5
CHUNK = 128
CONV_CH = D_INNER + 2 * SSD_GROUPS * D_STATE
K0 = ATTN_W
V0 = K0 + KV_W
Z0 = V0 + KV_W
XBC0 = Z0 + D_INNER
DT0 = XBC0 + CONV_CH
IN_COLS = DT0 + 2 * SSD_HEADS
N_EXPERTS = 16
CAPACITY_FACTOR = 2
D_EXPERT = 1024
PLE_DIM = 256
DN_ALPHA = (2.0 * DEPTH) ** 0.25
NORM_EPS = 1e-6
LN_EPS = 1e-5

LANES = 128
BF16_SUBLANES = 16
IN_COLS_PAD = DT0 + LANES
TOKEN_TILE = 512
ATTN_Q_TILE = 256
FFN_ROWS = 256
COMB_TILE = 128
COMB_WIN = COMB_TILE + BF16_SUBLANES
SEL_SLOT_CHUNK = 1024
VMEM_LIMIT = 56 * 1024 * 1024

HIGHEST = lax.Precision.HIGHEST


def _cparams(sem):
    return pltpu.CompilerParams(dimension_semantics=sem, vmem_limit_bytes=VMEM_LIMIT)


def _layer_norm_rows(x, g, b):
    xc = x - jnp.mean(x, axis=-1, keepdims=True)
    return xc * lax.rsqrt(jnp.mean(xc * xc, axis=-1, keepdims=True) + LN_EPS) * g + b


def _sigmoid(x):
    return 1.0 / (1.0 + jnp.exp(-x))


def _dot(a, b):
    return jnp.dot(a, b, preferred_element_type=F32)


def _dot_nt(a, b):
    return lax.dot_general(a, b, (((1,), (1,)), ((), ())), preferred_element_type=F32)


def _dot_exact(a, b):
    return jnp.dot(a, b, preferred_element_type=F32, precision=HIGHEST)


def _split_bf16(x):
    hi = x.astype(BF16)
    lo = (x - hi.astype(F32)).astype(BF16)
    return hi, lo


def _rope_swap(x):
    outs = []
    for c in range(x.shape[-1] // LANES):
        xc = x[:, c * LANES:(c + 1) * LANES]
        lane = lax.broadcasted_iota(I32, xc.shape, 1)
        up = pltpu.roll(xc, LANES - 16, axis=1)
        dn = pltpu.roll(xc, 16, axis=1)
        outs.append(jnp.where((lane % 32) < 16, up, dn))
    return outs[0] if len(outs) == 1 else jnp.concatenate(outs, axis=-1)


def _head_rms(x, w):
    width = x.shape[-1]
    r = lax.broadcasted_iota(I32, (width, width), 0) // HEAD_DIM
    c = lax.broadcasted_iota(I32, (width, width), 1) // HEAD_DIM
    blockdiag = jnp.where(r == c, 1.0, 0.0).astype(BF16)
    hi, lo = _split_bf16(x * x)
    ssq = _dot(hi, blockdiag) + _dot(lo, blockdiag)
    return x * lax.rsqrt(ssq * (1.0 / HEAD_DIM) + NORM_EPS) * w


def _inproj_kernel(x_ref, g_ref, b_ref, w_ref, qw_ref, kw_ref, cos_ref, sin_ref,
                   q_ref, k_ref, v_ref, z_ref, xbc_ref, dt_ref):
    xln = _layer_norm_rows(x_ref[...], g_ref[...], b_ref[...])
    proj = _dot(xln.astype(BF16), w_ref[...])
    cos = cos_ref[...]
    sin = sin_ref[...]
    q = _head_rms(proj[:, :K0], qw_ref[...])
    cos_q = jnp.concatenate([cos] * (ATTN_W // LANES), axis=-1)
    sin_q = jnp.concatenate([sin] * (ATTN_W // LANES), axis=-1)
    q = (q * cos_q + _rope_swap(q) * sin_q) * (HEAD_DIM ** -0.5)
    q_ref[...] = q.astype(BF16)
    k = _head_rms(proj[:, K0:V0], kw_ref[...])
    k = (k * cos + _rope_swap(k) * sin).astype(BF16)
    v = proj[:, V0:Z0].astype(BF16)
    for g in range(N_KV_HEADS):
        k_ref[g] = k[:, g * HEAD_DIM:(g + 1) * HEAD_DIM]
        v_ref[g] = v[:, g * HEAD_DIM:(g + 1) * HEAD_DIM]
    z_ref[...] = proj[:, Z0:XBC0]
    xbc_ref[...] = proj[:, XBC0:DT0]
    dt = proj[:, DT0:DT0 + LANES]
    for d in range(2):
        dt_ref[d] = dt[:, d * SSD_HEADS:(d + 1) * SSD_HEADS]


def _rope_tables(seq):
    rows = seq // GRID_W
    row_id = jnp.broadcast_to(jnp.arange(rows)[:, None], (rows, GRID_W)).reshape(seq)
    col_id = jnp.broadcast_to(jnp.arange(GRID_W)[None, :], (rows, GRID_W)).reshape(seq)
    pos = jnp.stack([row_id, col_id], axis=-1).astype(F32)
    n_freq = HEAD_DIM // 4
    inv_freq = ROPE_THETA ** (-jnp.arange(n_freq, dtype=F32) / n_freq)
    ang = pos[:, :, None] * inv_freq
    cos = jnp.broadcast_to(jnp.cos(ang)[:, :, None, :], (seq, 2, 2, n_freq)).reshape(seq, HEAD_DIM)
    sin = jnp.broadcast_to(jnp.sin(ang)[:, :, None, :], (seq, 2, 2, n_freq))
    sin = (sin * jnp.array([-1.0, 1.0], F32)[None, None, :, None]).reshape(seq, HEAD_DIM)
    reps = LANES // HEAD_DIM
    return jnp.tile(cos, (1, reps)), jnp.tile(sin, (1, reps))


def _inproj_call(x2d, seq, ln_g, ln_b, w_in_pad, qw, kw):
    tokens = x2d.shape[0]
    tm = TOKEN_TILE
    nt = tokens // tm
    tiles_per_seq = seq // tm
    cos, sin = _rope_tables(seq)
    row = lambda i: (i, 0)
    fixed = lambda i: (0, 0)
    return pl.pallas_call(
        _inproj_kernel,
        grid=(nt,),
        in_specs=[
            pl.BlockSpec((tm, D_MODEL), row),
            pl.BlockSpec((1, D_MODEL), fixed),
            pl.BlockSpec((1, D_MODEL), fixed),
            pl.BlockSpec((D_MODEL, IN_COLS_PAD), fixed),
            pl.BlockSpec((1, ATTN_W), fixed),
            pl.BlockSpec((1, KV_W), fixed),
            pl.BlockSpec((tm, LANES), lambda i: (i % tiles_per_seq, 0)),
            pl.BlockSpec((tm, LANES), lambda i: (i % tiles_per_seq, 0)),
        ],
        out_specs=[
            pl.BlockSpec((tm, ATTN_W), row),
            pl.BlockSpec((N_KV_HEADS, tm, HEAD_DIM), lambda i: (0, i, 0)),
            pl.BlockSpec((N_KV_HEADS, tm, HEAD_DIM), lambda i: (0, i, 0)),
            pl.BlockSpec((tm, D_INNER), row),
            pl.BlockSpec((tm, CONV_CH), row),
            pl.BlockSpec((2, tm, SSD_HEADS), lambda i: (0, i, 0)),
        ],
        out_shape=[
            jax.ShapeDtypeStruct((tokens, ATTN_W), BF16),
            jax.ShapeDtypeStruct((N_KV_HEADS, tokens, HEAD_DIM), BF16),
            jax.ShapeDtypeStruct((N_KV_HEADS, tokens, HEAD_DIM), BF16),
            jax.ShapeDtypeStruct((tokens, D_INNER), F32),
            jax.ShapeDtypeStruct((tokens, CONV_CH), F32),
            jax.ShapeDtypeStruct((2, tokens, SSD_HEADS), F32),
        ],
        compiler_params=_cparams(("parallel",)),
        name="inproj",
    )(x2d, ln_g, ln_b, w_in_pad, qw, kw, cos, sin)


def _attn_kernel(q_ref, k_ref, v_ref, o_ref):
    k = k_ref[...]
    v = v_ref[...]
    q = q_ref[...]
    outs = []
    for h in range(Q_PER_KV):
        s = _dot_nt(q[:, h * HEAD_DIM:(h + 1) * HEAD_DIM], k)
        m = jnp.max(s, axis=-1, keepdims=True)
        p = jnp.exp(s - m)
        l = jnp.sum(p, axis=-1, keepdims=True)
        o = _dot(p.astype(BF16), v)
        outs.append(o / l)
    o_ref[...] = jnp.concatenate(outs, axis=-1).astype(BF16)


def _attn_call(q, k, v, batch, seq):
    tokens = batch * seq
    tq = ATTN_Q_TILE
    nq = seq // tq
    width = Q_PER_KV * HEAD_DIM
    return pl.pallas_call(
        _attn_kernel,
        grid=(batch, N_KV_HEADS, nq),
        in_specs=[
            pl.BlockSpec((tq, width), lambda b, g, i: (b * nq + i, g)),
            pl.BlockSpec((None, seq, HEAD_DIM), lambda b, g, i: (g, b, 0)),
            pl.BlockSpec((None, seq, HEAD_DIM), lambda b, g, i: (g, b, 0)),
        ],
        out_specs=pl.BlockSpec((tq, width), lambda b, g, i: (b * nq + i, g)),
        out_shape=jax.ShapeDtypeStruct((tokens, ATTN_W), BF16),
        compiler_params=_cparams(("parallel", "parallel", "parallel")),
        name="attn",
    )(q, k, v)


HALO = 8


def _ssd_kernel(x_ref, prev_ref, next_ref, dt_ref, dtt_ref, cw_ref, cb_ref, dtb_ref, dtbt_ref,
                na_ref, nat_ref, dskip_ref, y_ref, pad_ref, state_ref, *, n_chunks):
    d = pl.program_id(1)
    c = pl.program_id(2)
    c_eff = jnp.where(d == 0, c, n_chunks - 1 - c)

    @pl.when(c == 0)
    def _():
        state_ref[...] = jnp.zeros_like(state_ref)

    has_prev = (c_eff > 0).astype(F32)
    has_next = (c_eff < n_chunks - 1).astype(F32)
    pad_ref[0:HALO, :] = prev_ref[...] * has_prev
    pad_ref[HALO:HALO + CHUNK, :] = x_ref[...]
    pad_ref[HALO + CHUNK:, :] = next_ref[...] * has_next
    cw = cw_ref[...]
    acc = jnp.broadcast_to(cb_ref[...], (CHUNK, CONV_CH))
    for tap in range(D_CONV):
        off = HALO + tap - D_CONV // 2
        acc = acc + pad_ref[off:off + CHUNK, :] * cw[tap:tap + 1, :]
    xc = acc * _sigmoid(acc)
    xs = xc[:, :D_INNER]
    b_all = xc[:, D_INNER:D_INNER + SSD_GROUPS * D_STATE]
    c_all = xc[:, D_INNER + SSD_GROUPS * D_STATE:]

    dt = jax.nn.softplus(dt_ref[...] + dtb_ref[...])
    a = dt * na_ref[...]
    a_row = jax.nn.softplus(dtt_ref[...] + dtbt_ref[...]) * nat_ref[...]

    ri = lax.broadcasted_iota(I32, (CHUNK, CHUNK), 0)
    ci = lax.broadcasted_iota(I32, (CHUNK, CHUNK), 1)
    fwd = d == 0
    ahead = jnp.where(fwd, ri - ci, ci - ri)
    causal = ahead >= 0
    tri = jnp.where(causal, 1.0, 0.0).astype(F32)
    tri_t = jnp.where(ahead <= 0, 1.0, 0.0).astype(F32)
    cum = _dot_exact(tri, a)
    cum_row = _dot_exact(a_row, tri_t)
    total = jnp.sum(a, axis=0, keepdims=True)

    er = lax.broadcasted_iota(I32, (SSD_HEADS, D_INNER), 0)
    ec = lax.broadcasted_iota(I32, (SSD_HEADS, D_INNER), 1) // SSD_HEAD_DIM
    expand = jnp.where(er == ec, 1.0, 0.0).astype(F32)
    dt_e = _dot_exact(dt, expand)
    dec_state_e = _dot_exact(jnp.exp(total - cum), expand)
    dec_out_e = _dot_exact(jnp.exp(cum), expand)
    dec_chunk_e = _dot_exact(jnp.exp(total), expand)

    xdt = xs * dt_e
    xdt_bf = xdt.astype(BF16)
    xdec_bf = (xdt * dec_state_e).astype(BF16)

    ys = []
    for g in range(SSD_GROUPS):
        bg = b_all[:, g * D_STATE:(g + 1) * D_STATE]
        cg = c_all[:, g * D_STATE:(g + 1) * D_STATE].astype(BF16)
        cb = _dot_nt(cg, bg.astype(BF16))
        lo = g * HEADS_PER_GROUP * SSD_HEAD_DIM
        hi = lo + HEADS_PER_GROUP * SSD_HEAD_DIM
        prev = state_ref[g]
        y_off = _dot(cg, prev.astype(BF16)) * dec_out_e[:, lo:hi]
        y_heads = []
        for r in range(HEADS_PER_GROUP):
            h = g * HEADS_PER_GROUP + r
            seg = cum[:, h:h + 1] - cum_row[h:h + 1, :]
            m = jnp.where(causal, cb * jnp.exp(jnp.where(causal, seg, 0.0)), 0.0)
            y_heads.append(_dot(m.astype(BF16), xdt_bf[:, h * SSD_HEAD_DIM:(h + 1) * SSD_HEAD_DIM]))
        ys.append(jnp.concatenate(y_heads, axis=-1) + y_off)
        st = _dot(bg.T.astype(BF16), xdec_bf[:, lo:hi])
        state_ref[g] = prev * dec_chunk_e[:, lo:hi] + st
    y = jnp.concatenate(ys, axis=-1)
    y_ref[...] = y + jnp.where(fwd, 1.0, 0.0) * (dskip_ref[...] * xs)


def _ssd_call(xbc, dt2, batch, seq, conv_w, conv_b, dt_bias, a_log, d_skip):
    tokens = batch * seq
    nc = seq // CHUNK
    halo_per_chunk = CHUNK // HALO
    n_halo = tokens // HALO
    dtt = dt2.reshape(2, batch, seq, SSD_HEADS).transpose(1, 0, 3, 2).reshape(batch * 2 * SSD_HEADS, seq)
    cw = jnp.zeros((8, CONV_CH), F32).at[:D_CONV].set(conv_w)
    cb = conv_b.reshape(1, CONV_CH)
    dtb = dt_bias.reshape(2, 1, SSD_HEADS)
    dtbt = dt_bias.reshape(2, SSD_HEADS, 1)
    neg_a = -jnp.exp(a_log)
    na = neg_a.reshape(2, 1, SSD_HEADS)
    nat = neg_a.reshape(2, SSD_HEADS, 1)
    dskip_e = jnp.repeat(d_skip, SSD_HEAD_DIM).reshape(1, D_INNER)

    def ceff(d, c):
        return jnp.where(d == 0, c, nc - 1 - c)

    def main_idx(b, d, c):
        return (b * nc + ceff(d, c), 0)

    def prev_idx(b, d, c):
        return (jnp.maximum((b * nc + ceff(d, c)) * halo_per_chunk - 1, 0), 0)

    def next_idx(b, d, c):
        return (jnp.minimum((b * nc + ceff(d, c) + 1) * halo_per_chunk, n_halo - 1), 0)

    sel_d = lambda b, d, c: (d, 0, 0)
    fixed = lambda b, d, c: (0, 0)
    return pl.pallas_call(
        functools.partial(_ssd_kernel, n_chunks=nc),
        grid=(batch, 2, nc),
        in_specs=[
            pl.BlockSpec((CHUNK, CONV_CH), main_idx),
            pl.BlockSpec((HALO, CONV_CH), prev_idx),
            pl.BlockSpec((HALO, CONV_CH), next_idx),
            pl.BlockSpec((None, CHUNK, SSD_HEADS), lambda b, d, c: (d, b * nc + ceff(d, c), 0)),
            pl.BlockSpec((SSD_HEADS, CHUNK), lambda b, d, c: (b * 2 + d, ceff(d, c))),
            pl.BlockSpec((8, CONV_CH), fixed),
            pl.BlockSpec((1, CONV_CH), fixed),
            pl.BlockSpec((None, 1, SSD_HEADS), sel_d),
            pl.BlockSpec((None, SSD_HEADS, 1), sel_d),
            pl.BlockSpec((None, 1, SSD_HEADS), sel_d),
            pl.BlockSpec((None, SSD_HEADS, 1), sel_d),
            pl.BlockSpec((1, D_INNER), fixed),
        ],
        out_specs=pl.BlockSpec((None, CHUNK, D_INNER), lambda b, d, c: (d, b * nc + ceff(d, c), 0)),
        out_shape=jax.ShapeDtypeStruct((2, tokens, D_INNER), F32),
        scratch_shapes=[
            pltpu.VMEM((CHUNK + 2 * HALO, CONV_CH), F32),
            pltpu.VMEM((SSD_GROUPS, D_STATE, HEADS_PER_GROUP * SSD_HEAD_DIM), F32),
        ],
        compiler_params=_cparams(("parallel", "arbitrary", "arbitrary")),
        name="ssd",
    )(xbc, xbc, xbc, dt2, dtt, cw, cb, dtb, dtbt, na, nat, dskip_e)


def _mix_kernel(x_ref, ing_ref, inb_ref, attn_ref, yf_ref, yb_ref, z_ref, nw_ref, wo_ref,
                g_ref, b_ref, wrh_ref, wrl_ref, x1_ref, aff_ref):
    xln = _layer_norm_rows(x_ref[...], ing_ref[...], inb_ref[...])
    z = z_ref[...]
    y = (yf_ref[...] + yb_ref[...]) * (z * _sigmoid(z))
    gw = D_INNER // SSD_GROUPS
    parts = []
    for g in range(SSD_GROUPS):
        yg = y[:, g * gw:(g + 1) * gw]
        parts.append(yg * lax.rsqrt(jnp.mean(yg * yg, axis=-1, keepdims=True) + NORM_EPS))
    ssd = (jnp.concatenate(parts, axis=-1) * nw_ref[...]).astype(BF16)
    mix = _dot(jnp.concatenate([attn_ref[...], ssd], axis=-1), wo_ref[...])
    x1 = _layer_norm_rows(DN_ALPHA * xln + mix, g_ref[...], b_ref[...])
    x1_ref[...] = x1
    hi, lo = _split_bf16(x1)
    wrh = wrh_ref[...]
    logits = _dot(hi, wrh) + _dot(lo, wrh) + _dot(hi, wrl_ref[...])
    lane = lax.broadcasted_iota(I32, logits.shape, 1)
    logits = jnp.where(lane < N_EXPERTS, logits, -jnp.inf)
    e = jnp.exp(logits - jnp.max(logits, axis=-1, keepdims=True))
    aff = e / jnp.sum(e, axis=-1, keepdims=True)
    aff_ref[...] = aff[:, :N_EXPERTS]


def _mix_call(x2d, ln_in_g, ln_in_b, attn, y2, z, norm_w, w_out, ln_g, ln_b, wr_hi, wr_lo):
    tokens = x2d.shape[0]
    tm = TOKEN_TILE
    row = lambda i: (i, 0)
    fixed = lambda i: (0, 0)
    vec = pl.BlockSpec((1, D_MODEL), fixed)
    return pl.pallas_call(
        _mix_kernel,
        grid=(tokens // tm,),
        in_specs=[
            pl.BlockSpec((tm, D_MODEL), row), vec, vec,
            pl.BlockSpec((tm, ATTN_W), row),
            pl.BlockSpec((None, tm, D_INNER), lambda i: (0, i, 0)),
            pl.BlockSpec((None, tm, D_INNER), lambda i: (1, i, 0)),
            pl.BlockSpec((tm, D_INNER), row),
            pl.BlockSpec((1, D_INNER), fixed),
            pl.BlockSpec((D_MODEL, D_MODEL), fixed),
            vec, vec,
            pl.BlockSpec((D_MODEL, LANES), fixed),
            pl.BlockSpec((D_MODEL, LANES), fixed),
        ],
        out_specs=[
            pl.BlockSpec((tm, D_MODEL), row),
            pl.BlockSpec((tm, N_EXPERTS), row),
        ],
        out_shape=[
            jax.ShapeDtypeStruct((tokens, D_MODEL), F32),
            jax.ShapeDtypeStruct((tokens, N_EXPERTS), F32),
        ],
        compiler_params=_cparams(("parallel",)),
        name="mix",
    )(x2d, ln_in_g, ln_in_b, attn, y2, y2, z, norm_w, w_out, ln_g, ln_b, wr_hi, wr_lo)


def _tile_cumsum(mask_bf, upper, strict_lower):
    wloc = _dot(mask_bf, upper)
    tot = jnp.broadcast_to(wloc[:, LANES - 1:LANES], wloc.shape).astype(BF16)
    offs = _dot(strict_lower, tot)
    return wloc, offs


def _select_kernel(aff_ref, idx_ref, lp_ref, win_ref, *, cap):
    nt = aff_ref.shape[0]
    bits = pltpu.bitcast(aff_ref[...], I32)

    t = jnp.zeros((1, 1), I32)
    for bit in range(30, -1, -1):
        cand = t | (1 << bit)
        cnt = jnp.sum(jnp.where(bits >= cand, 1.0, 0.0), keepdims=True)
        t = jnp.where(cnt >= cap, cand, t)
    gt = bits > t
    eq = bits == t
    need = cap - jnp.sum(jnp.where(gt, 1.0, 0.0), keepdims=True)

    ri = lax.broadcasted_iota(I32, (LANES, LANES), 0)
    ci = lax.broadcasted_iota(I32, (LANES, LANES), 1)
    upper = jnp.where(ri <= ci, 1.0, 0.0).astype(BF16)
    rt = lax.broadcasted_iota(I32, (nt, nt), 0)
    ct = lax.broadcasted_iota(I32, (nt, nt), 1)
    strict_lower = jnp.where(ct < rt, 1.0, 0.0).astype(BF16)

    eq_loc, eq_offs = _tile_cumsum(jnp.where(eq, 1.0, 0.0).astype(BF16), upper, strict_lower)
    sel = gt | (eq & (eq_loc + eq_offs <= need))
    sel_bf = jnp.where(sel, 1.0, 0.0).astype(BF16)
    wloc, offs = _tile_cumsum(sel_bf, upper, strict_lower)

    win = jnp.minimum(jnp.floor(offs * (1.0 / BF16_SUBLANES)) * BF16_SUBLANES, float(cap - COMB_WIN))
    win_ref[...] = win
    lp_ref[...] = jnp.where(sel, wloc + offs - 1.0 - win, -1.0)

    tot_row = _dot_nt(jnp.ones((8, LANES), BF16), sel_bf)
    upper_t = jnp.where(rt <= ct, 1.0, 0.0).astype(BF16)
    incl_row = _dot(tot_row.astype(BF16), upper_t)[0:1, :]
    excl_row = incl_row - tot_row[0:1, :]
    tile_id = lax.broadcasted_iota(I32, (nt, LANES), 0)
    lane_id = lax.broadcasted_iota(I32, (nt, LANES), 1)
    offs_hi = jnp.floor(offs * (1.0 / LANES))
    side = jnp.where(lane_id == 0, (tile_id // 2).astype(F32),
           jnp.where(lane_id == 1, (tile_id % 2).astype(F32),
           jnp.where(lane_id == 2, offs_hi,
           jnp.where(lane_id == 3, offs - offs_hi * LANES, 0.0))))
    table = jnp.concatenate([wloc, side], axis=-1).astype(BF16)
    for s0 in range(0, cap, SEL_SLOT_CHUNK):
        n = min(SEL_SLOT_CHUNK, cap - s0)
        slot = (lax.broadcasted_iota(I32, (n, 1), 0) + s0).astype(F32)
        onehot = jnp.where((excl_row <= slot) & (slot < incl_row), 1.0, 0.0).astype(BF16)
        got = _dot(onehot, table)
        tile = 2.0 * got[:, LANES:LANES + 1] + got[:, LANES + 1:LANES + 2]
        base = LANES * got[:, LANES + 2:LANES + 3] + got[:, LANES + 3:LANES + 4]
        lane = jnp.sum(jnp.where(got[:, :LANES] <= slot - base, 1.0, 0.0), axis=-1, keepdims=True)
        tok = jnp.broadcast_to(tile * LANES + lane, (n, LANES))
        idx_ref[:, s0:s0 + n] = tok.T[0:1, :].astype(I32)


def _select_call(aff):
    tokens = aff.shape[0]
    cap = CAPACITY_FACTOR * tokens // N_EXPERTS
    nt = tokens // LANES
    aff_t = aff.T.reshape(N_EXPERTS, nt, LANES)
    blk = pl.BlockSpec((None, nt, LANES), lambda e: (e, 0, 0))
    idx, lp, win = pl.pallas_call(
        functools.partial(_select_kernel, cap=cap),
        grid=(N_EXPERTS,),
        in_specs=[blk],
        out_specs=[pl.BlockSpec((None, 1, cap), lambda e: (e, 0, 0)), blk, blk],
        out_shape=[
            jax.ShapeDtypeStruct((N_EXPERTS, 1, cap), I32),
            jax.ShapeDtypeStruct((N_EXPERTS, nt, LANES), F32),
            jax.ShapeDtypeStruct((N_EXPERTS, nt, LANES), F32),
        ],
        compiler_params=_cparams(("parallel",)),
        name="select",
    )(aff_t)
    lp_tok = lp.reshape(N_EXPERTS, tokens).T
    win_start = win[:, :, 0].astype(I32).reshape(N_EXPERTS * nt)
    return idx.reshape(N_EXPERTS * cap), lp_tok, win_start, cap


def _ffn_kernel(idx_ref, x_hbm, wg_ref, wu_ref, wd_ref, ye_ref, buf_ref, sem_ref, *, cap):
    e = pl.program_id(0)
    c = pl.program_id(1)
    n_chunks = cap // FFN_ROWS
    step = e * n_chunks + c
    n_steps = N_EXPERTS * n_chunks
    slot = step % 2

    def row_copy(base, r, buf_slot):
        tok = idx_ref[base + r]
        return pltpu.make_async_copy(x_hbm.at[pl.ds(tok, 1), :],
                                     buf_ref.at[buf_slot, pl.ds(r, 1), :],
                                     sem_ref.at[buf_slot])

    def start_chunk(chunk_step, buf_slot):
        base = chunk_step * FFN_ROWS

        def body(r, carry):
            row_copy(base, r, buf_slot).start()
            return carry
        lax.fori_loop(0, FFN_ROWS, body, 0, unroll=8)

    @pl.when(step == 0)
    def _():
        start_chunk(0, 0)

    @pl.when(step + 1 < n_steps)
    def _():
        start_chunk(step + 1, 1 - slot)

    def wait_body(r, carry):
        row_copy(step * FFN_ROWS, r, slot).wait()
        return carry
    lax.fori_loop(0, FFN_ROWS, wait_body, 0, unroll=8)

    xe = buf_ref[slot].astype(BF16)
    gate = _dot(xe, wg_ref[...])
    up = _dot(xe, wu_ref[...])
    hid = (gate * _sigmoid(gate) * up).astype(BF16)
    ye_ref[...] = _dot(hid, wd_ref[...]).astype(BF16)


def _ffn_call(idx, x1, wg, wu, wd, cap):
    n_chunks = cap // FFN_ROWS
    wspec = lambda d0, d1: pl.BlockSpec((None, d0, d1), lambda e, c, idx: (e, 0, 0))
    grid_spec = pltpu.PrefetchScalarGridSpec(
        num_scalar_prefetch=1,
        grid=(N_EXPERTS, n_chunks),
        in_specs=[
            pl.BlockSpec(memory_space=pl.ANY),
            wspec(D_MODEL, D_EXPERT), wspec(D_MODEL, D_EXPERT), wspec(D_EXPERT, D_MODEL),
        ],
        out_specs=pl.BlockSpec((None, FFN_ROWS, D_MODEL), lambda e, c, idx: (e, c, 0)),
        scratch_shapes=[
            pltpu.VMEM((2, FFN_ROWS, D_MODEL), F32),
            pltpu.SemaphoreType.DMA((2,)),
        ],
    )
    return pl.pallas_call(
        functools.partial(_ffn_kernel, cap=cap),
        grid_spec=grid_spec,
        out_shape=jax.ShapeDtypeStruct((N_EXPERTS, cap, D_MODEL), BF16),
        compiler_params=_cparams(("arbitrary", "arbitrary")),
        name="ffn",
    )(idx, x1, wg, wu, wd)


def _combine_kernel(win_ref, x1_ref, p_ref, lp_ref, aff_ref, ye_hbm, wpg_ref, bpg_ref, wple_ref,
                    g_ref, b_ref, o_ref, buf_ref, sem_ref, *, n_tiles):
    i = pl.program_id(0)
    slot = i % 2

    def win_copy(tile, e, buf_slot):
        start = pl.multiple_of(win_ref[e * n_tiles + tile], BF16_SUBLANES)
        return pltpu.make_async_copy(ye_hbm.at[e, pl.ds(start, COMB_WIN), :],
                                     buf_ref.at[buf_slot, e], sem_ref.at[buf_slot, e])

    def start_tile(tile, buf_slot):
        for e in range(N_EXPERTS):
            win_copy(tile, e, buf_slot).start()

    @pl.when(i == 0)
    def _():
        start_tile(0, 0)

    @pl.when(i + 1 < n_tiles)
    def _():
        start_tile(i + 1, 1 - slot)

    x1 = x1_ref[...]
    ple = _sigmoid(_dot(x1.astype(BF16), wpg_ref[...]) + bpg_ref[...]) \
        * _dot(p_ref[...].astype(BF16), wple_ref[...])
    acc = DN_ALPHA * x1 + ple
    lp = lp_ref[...]
    aff = aff_ref[...]
    slot_iota = lax.broadcasted_iota(I32, (COMB_TILE, COMB_WIN), 1).astype(F32)
    for e in range(N_EXPERTS):
        win_copy(i, e, slot).wait()
        onehot = jnp.where(lp[:, e:e + 1] == slot_iota, 1.0, 0.0).astype(BF16)
        acc = acc + aff[:, e:e + 1] * _dot(onehot, buf_ref[slot, e])
    o_ref[...] = _layer_norm_rows(acc, g_ref[...], b_ref[...])


def _combine_call(win_start, x1, p2d, lp_tok, aff, ye, wpg, bpg, wple, ln_g, ln_b):
    tokens = x1.shape[0]
    n_tiles = tokens // COMB_TILE
    row = lambda i, w: (i, 0)
    fixed = lambda i, w: (0, 0)
    vec = pl.BlockSpec((1, D_MODEL), fixed)
    grid_spec = pltpu.PrefetchScalarGridSpec(
        num_scalar_prefetch=1,
        grid=(n_tiles,),
        in_specs=[
            pl.BlockSpec((COMB_TILE, D_MODEL), row),
            pl.BlockSpec((COMB_TILE, PLE_DIM), row),
            pl.BlockSpec((COMB_TILE, N_EXPERTS), row),
            pl.BlockSpec((COMB_TILE, N_EXPERTS), row),
            pl.BlockSpec(memory_space=pl.ANY),
            pl.BlockSpec((D_MODEL, D_MODEL), fixed),
            vec,
            pl.BlockSpec((PLE_DIM, D_MODEL), fixed),
            vec, vec,
        ],
        out_specs=pl.BlockSpec((COMB_TILE, D_MODEL), row),
        scratch_shapes=[
            pltpu.VMEM((2, N_EXPERTS, COMB_WIN, D_MODEL), BF16),
            pltpu.SemaphoreType.DMA((2, N_EXPERTS)),
        ],
    )
    return pl.pallas_call(
        functools.partial(_combine_kernel, n_tiles=n_tiles),
        grid_spec=grid_spec,
        out_shape=jax.ShapeDtypeStruct((tokens, D_MODEL), F32),
        compiler_params=_cparams(("arbitrary",)),
        name="combine",
    )(win_start, x1, p2d, lp_tok, aff, ye, wpg, bpg, wple, ln_g, ln_b)


def _prep_weights(ln_in_g, ln_in_b, w_in, q_norm_w, k_norm_w, conv_w, conv_b, dt_bias, a_log,
                  d_skip, ssd_norm_w, w_out, ln1_g, ln1_b, w_router, w_e_gate, w_e_up, w_e_down,
                  w_ple_gate, b_ple_gate, w_ple, ln2_g, ln2_b):
    l = 0
    vec = lambda a: a.reshape(1, -1)
    w_in_pad = jnp.zeros((D_MODEL, IN_COLS_PAD), F32).at[:, :IN_COLS].set(w_in[l]).astype(BF16)
    wr = jnp.zeros((D_MODEL, LANES), F32).at[:, :N_EXPERTS].set(w_router[l])
    wr_hi, wr_lo = _split_bf16(wr)
    return dict(
        ln_in_g=vec(ln_in_g), ln_in_b=vec(ln_in_b), w_in=w_in_pad,
        qw=vec(jnp.tile(q_norm_w[l], N_Q_HEADS)), kw=vec(jnp.tile(k_norm_w[l], N_KV_HEADS)),
        conv_w=conv_w[l], conv_b=conv_b[l], dt_bias=dt_bias[l], a_log=a_log[l], d_skip=d_skip[l],
        norm_w=vec(ssd_norm_w[l]), w_out=w_out[l].astype(BF16), ln1_g=vec(ln1_g[l]), ln1_b=vec(ln1_b[l]),
        wr_hi=wr_hi, wr_lo=wr_lo,
        wg=w_e_gate[l].astype(BF16), wu=w_e_up[l].astype(BF16), wd=w_e_down[l].astype(BF16),
        wpg=w_ple_gate[l].astype(BF16), bpg=vec(b_ple_gate[l]), wple=w_ple[l].astype(BF16),
        ln2_g=vec(ln2_g[l]), ln2_b=vec(ln2_b[l]),
    )


def _trunk(x, p, w):
    batch, seq, _ = x.shape
    tokens = batch * seq
    x2d = x.reshape(tokens, D_MODEL)
    q, k, v, z, xbc, dt2 = _inproj_call(x2d, seq, w["ln_in_g"], w["ln_in_b"], w["w_in"], w["qw"], w["kw"])
    attn = _attn_call(q, k, v, batch, seq)
    y2 = _ssd_call(xbc, dt2, batch, seq, w["conv_w"], w["conv_b"], w["dt_bias"], w["a_log"], w["d_skip"])
    x1, aff = _mix_call(x2d, w["ln_in_g"], w["ln_in_b"], attn, y2, z, w["norm_w"], w["w_out"],
                        w["ln1_g"], w["ln1_b"], w["wr_hi"], w["wr_lo"])
    idx, lp_tok, win_start, cap = _select_call(aff)
    ye = _ffn_call(idx, x1, w["wg"], w["wu"], w["wd"], cap)
    out = _combine_call(win_start, x1, p[0].reshape(tokens, PLE_DIM), lp_tok, aff, ye,
                        w["wpg"], w["bpg"], w["wple"], w["ln2_g"], w["ln2_b"])
    return out.reshape(batch, seq, D_MODEL)


def kernel(x_prompt, x_sample, p_prompt, p_sample, ln_in_g, ln_in_b, w_in, q_norm_w, k_norm_w, conv_w, conv_b, dt_bias, a_log, d_skip, ssd_norm_w, w_out, ln1_g, ln1_b, w_router, w_e_gate, w_e_up, w_e_down, w_ple_gate, b_ple_gate, w_ple, ln2_g, ln2_b):
    w = _prep_weights(ln_in_g, ln_in_b, w_in, q_norm_w, k_norm_w, conv_w, conv_b, dt_bias, a_log,
                      d_skip, ssd_norm_w, w_out, ln1_g, ln1_b, w_router, w_e_gate, w_e_up, w_e_down,
                      w_ple_gate, b_ple_gate, w_ple, ln2_g, ln2_b)
    return (_trunk(x_prompt, p_prompt, w), _trunk(x_sample, p_sample, w))
```

```python
import functools
import math

import jax
import jax.numpy as jnp
from jax import lax
from jax.experimental import pallas as pl
from jax.experimental.pallas import tpu as pltpu

F32 = jnp.float32
BF16 = jnp.bfloat16
I32 = jnp.int32

D_MODEL = 1024
DEPTH = 1
GRID_W = 64
N_Q_HEADS = 8
N_KV_HEADS = 2
HEAD_DIM = 64
Q_PER_KV = N_Q_HEADS // N_KV_HEADS
ATTN_W = N_Q_HEADS * HEAD_DIM
KV_W = N_KV_HEADS * HEAD_DIM
ROPE_THETA = 10000.0
SSD_HEADS = 8
SSD_HEAD_DIM = 64
D_INNER = SSD_HEADS * SSD_HEAD_DIM
SSD_GROUPS = 2
HEADS_PER_GROUP = SSD_HEADS // SSD_GROUPS
D_STATE = 128
D_CONV = 5
CHUNK = 128
CONV_CH = D_INNER + 2 * SSD_GROUPS * D_STATE
K0 = ATTN_W
V0 = K0 + KV_W
Z0 = V0 + KV_W
XBC0 = Z0 + D_INNER
DT0 = XBC0 + CONV_CH
IN_COLS = DT0 + 2 * SSD_HEADS
N_EXPERTS = 16
CAPACITY_FACTOR = 2
D_EXPERT = 1024
PLE_DIM = 256
DN_ALPHA = (2.0 * DEPTH) ** 0.25
NORM_EPS = 1e-6
LN_EPS = 1e-5

LANES = 128
BF16_SUBLANES = 16
IN_COLS_PAD = DT0 + LANES
TOKEN_TILE = 512
ATTN_Q_TILE = 256
FFN_ROWS = 256
COMB_TILE = 256
COMB_WIN = 64
COMB_ROUNDS = -(-(COMB_TILE + BF16_SUBLANES - 1) // COMB_WIN)
COMB_SPAN = COMB_ROUNDS * COMB_WIN
SEL_SLOT_CHUNK = 1024
VMEM_LIMIT = 56 * 1024 * 1024


def _cparams(sem):
    return pltpu.CompilerParams(dimension_semantics=sem, vmem_limit_bytes=VMEM_LIMIT)


def _layer_norm_rows(x, g, b):
    xc = x - jnp.mean(x, axis=-1, keepdims=True)
    return xc * lax.rsqrt(jnp.mean(xc * xc, axis=-1, keepdims=True) + LN_EPS) * g + b


def _sigmoid(x):
    return 1.0 / (1.0 + jnp.exp(-x))


def _dot(a, b):
    return jnp.dot(a, b, preferred_element_type=F32)


def _dot_nt(a, b):
    return lax.dot_general(a, b, (((1,), (1,)), ((), ())), preferred_element_type=F32)


def _split_bf16(x):
    hi = x.astype(BF16)
    lo = (x - hi.astype(F32)).astype(BF16)
    return hi, lo


def _split3_bf16(x):
    hi = x.astype(BF16)
    rest = x - hi.astype(F32)
    mid = rest.astype(BF16)
    lo = (rest - mid.astype(F32)).astype(BF16)
    return hi, mid, lo


def _rope_swap(x):
    outs = []
    for c in range(x.shape[-1] // LANES):
        xc = x[:, c * LANES:(c + 1) * LANES]
        lane = lax.broadcasted_iota(I32, xc.shape, 1)
        up = pltpu.roll(xc, LANES - 16, axis=1)
        dn = pltpu.roll(xc, 16, axis=1)
        outs.append(jnp.where((lane % 32) < 16, up, dn))
    return outs[0] if len(outs) == 1 else jnp.concatenate(outs, axis=-1)


def _head_rms(x, w):
    width = x.shape[-1]
    r = lax.broadcasted_iota(I32, (width, width), 0) // HEAD_DIM
    c = lax.broadcasted_iota(I32, (width, width), 1) // HEAD_DIM
    blockdiag = jnp.where(r == c, 1.0, 0.0).astype(BF16)
    hi, lo = _split_bf16(x * x)
    ssq = _dot(hi, blockdiag) + _dot(lo, blockdiag)
    return x * lax.rsqrt(ssq * (1.0 / HEAD_DIM) + NORM_EPS) * w


def _inproj_kernel(x_ref, g_ref, b_ref, w_ref, qw_ref, kw_ref, cos_ref, sin_ref,
                   q_ref, k_ref, v_ref, z_ref, xbc_ref, dt_ref):
    xln = _layer_norm_rows(x_ref[...], g_ref[...], b_ref[...])
    proj = _dot(xln.astype(BF16), w_ref[...])
    cos = cos_ref[...]
    sin = sin_ref[...]
    q = _head_rms(proj[:, :K0], qw_ref[...])
    cos_q = jnp.concatenate([cos] * (ATTN_W // LANES), axis=-1)
    sin_q = jnp.concatenate([sin] * (ATTN_W // LANES), axis=-1)
    q = (q * cos_q + _rope_swap(q) * sin_q) * (HEAD_DIM ** -0.5)
    q_ref[...] = q.astype(BF16)
    k = _head_rms(proj[:, K0:V0], kw_ref[...])
    k = (k * cos + _rope_swap(k) * sin).astype(BF16)
    v = proj[:, V0:Z0]
    ones_col = jnp.where(lax.broadcasted_iota(I32, (v.shape[0], LANES - HEAD_DIM), 1) == 0, 1.0, 0.0)
    for g in range(N_KV_HEADS):
        k_ref[g] = k[:, g * HEAD_DIM:(g + 1) * HEAD_DIM]
        v_ref[g] = jnp.concatenate([v[:, g * HEAD_DIM:(g + 1) * HEAD_DIM], ones_col], axis=-1).astype(BF16)
    z_ref[...] = proj[:, Z0:XBC0]
    xbc_ref[...] = proj[:, XBC0:DT0]
    dt = proj[:, DT0:DT0 + LANES]
    for d in range(2):
        dt_ref[d] = dt[:, d * SSD_HEADS:(d + 1) * SSD_HEADS]


def _rope_tables(seq):
    rows = seq // GRID_W
    row_id = jnp.broadcast_to(jnp.arange(rows)[:, None], (rows, GRID_W)).reshape(seq)
    col_id = jnp.broadcast_to(jnp.arange(GRID_W)[None, :], (rows, GRID_W)).reshape(seq)
    pos = jnp.stack([row_id, col_id], axis=-1).astype(F32)
    n_freq = HEAD_DIM // 4
    inv_freq = ROPE_THETA ** (-jnp.arange(n_freq, dtype=F32) / n_freq)
    ang = pos[:, :, None] * inv_freq
    cos = jnp.broadcast_to(jnp.cos(ang)[:, :, None, :], (seq, 2, 2, n_freq)).reshape(seq, HEAD_DIM)
    sin = jnp.broadcast_to(jnp.sin(ang)[:, :, None, :], (seq, 2, 2, n_freq))
    sin = (sin * jnp.array([-1.0, 1.0], F32)[None, None, :, None]).reshape(seq, HEAD_DIM)
    reps = LANES // HEAD_DIM
    return jnp.tile(cos, (1, reps)), jnp.tile(sin, (1, reps))


def _inproj_call(x2d, seq, ln_g, ln_b, w_in_pad, qw, kw):
    tokens = x2d.shape[0]
    tm = TOKEN_TILE
    nt = tokens // tm
    tiles_per_seq = seq // tm
    cos, sin = _rope_tables(seq)
    row = lambda i: (i, 0)
    fixed = lambda i: (0, 0)
    return pl.pallas_call(
        _inproj_kernel,
        grid=(nt,),
        in_specs=[
            pl.BlockSpec((tm, D_MODEL), row),
            pl.BlockSpec((1, D_MODEL), fixed),
            pl.BlockSpec((1, D_MODEL), fixed),
            pl.BlockSpec((D_MODEL, IN_COLS_PAD), fixed),
            pl.BlockSpec((1, ATTN_W), fixed),
            pl.BlockSpec((1, KV_W), fixed),
            pl.BlockSpec((tm, LANES), lambda i: (i % tiles_per_seq, 0)),
            pl.BlockSpec((tm, LANES), lambda i: (i % tiles_per_seq, 0)),
        ],
        out_specs=[
            pl.BlockSpec((tm, ATTN_W), row),
            pl.BlockSpec((N_KV_HEADS, tm, HEAD_DIM), lambda i: (0, i, 0)),
            pl.BlockSpec((N_KV_HEADS, tm, LANES), lambda i: (0, i, 0)),
            pl.BlockSpec((tm, D_INNER), row),
            pl.BlockSpec((tm, CONV_CH), row),
            pl.BlockSpec((2, tm, SSD_HEADS), lambda i: (0, i, 0)),
        ],
        out_shape=[
            jax.ShapeDtypeStruct((tokens, ATTN_W), BF16),
            jax.ShapeDtypeStruct((N_KV_HEADS, tokens, HEAD_DIM), BF16),
            jax.ShapeDtypeStruct((N_KV_HEADS, tokens, LANES), BF16),
            jax.ShapeDtypeStruct((tokens, D_INNER), F32),
            jax.ShapeDtypeStruct((tokens, CONV_CH), F32),
            jax.ShapeDtypeStruct((2, tokens, SSD_HEADS), F32),
        ],
        compiler_params=_cparams(("parallel",)),
        name="inproj",
    )(x2d, ln_g, ln_b, w_in_pad, qw, kw, cos, sin)


def _attn_kernel(q_ref, k_ref, v_ref, o_ref):
    k = k_ref[...]
    v = v_ref[...]
    q = q_ref[...]
    outs = []
    for h in range(Q_PER_KV):
        s = _dot_nt(q[:, h * HEAD_DIM:(h + 1) * HEAD_DIM], k)
        m = jnp.max(s, axis=-1, keepdims=True)
        p = jnp.exp((s - m).astype(BF16))
        o = _dot(p, v)
        outs.append(o[:, :HEAD_DIM] / o[:, HEAD_DIM:HEAD_DIM + 1])
    o_ref[...] = jnp.concatenate(outs, axis=-1).astype(BF16)


def _attn_call(q, k, v, batch, seq):
    tokens = batch * seq
    tq = ATTN_Q_TILE
    nq = seq // tq
    width = Q_PER_KV * HEAD_DIM
    return pl.pallas_call(
        _attn_kernel,
        grid=(batch, N_KV_HEADS, nq),
        in_specs=[
            pl.BlockSpec((tq, width), lambda b, g, i: (b * nq + i, g)),
            pl.BlockSpec((None, seq, HEAD_DIM), lambda b, g, i: (g, b, 0)),
            pl.BlockSpec((None, seq, LANES), lambda b, g, i: (g, b, 0)),
        ],
        out_specs=pl.BlockSpec((tq, width), lambda b, g, i: (b * nq + i, g)),
        out_shape=jax.ShapeDtypeStruct((tokens, ATTN_W), BF16),
        compiler_params=_cparams(("parallel", "parallel", "parallel")),
        name="attn",
    )(q, k, v)


HALO = 8


def _ssd_kernel(x_ref, prev_ref, next_ref, dt_ref, dtt_ref, cw_ref, cb_ref, dtb_ref, dtbt_ref,
                alog_ref, alogt_ref, dskip_ref, y_ref, pad_ref, state_ref, *, n_chunks):
    d = pl.program_id(1)
    c = pl.program_id(2)
    c_eff = jnp.where(d == 0, c, n_chunks - 1 - c)

    @pl.when(c == 0)
    def _():
        state_ref[...] = jnp.zeros_like(state_ref)

    has_prev = (c_eff > 0).astype(F32)
    has_next = (c_eff < n_chunks - 1).astype(F32)
    pad_ref[0:HALO, :] = prev_ref[...] * has_prev
    pad_ref[HALO:HALO + CHUNK, :] = x_ref[...]
    pad_ref[HALO + CHUNK:, :] = next_ref[...] * has_next
    cw = cw_ref[...]
    acc = jnp.broadcast_to(cb_ref[...], (CHUNK, CONV_CH))
    for tap in range(D_CONV):
        off = HALO + tap - D_CONV // 2
        acc = acc + pad_ref[off:off + CHUNK, :] * cw[tap:tap + 1, :]
    xc = acc * _sigmoid(acc)
    xs = xc[:, :D_INNER]
    b_all = xc[:, D_INNER:D_INNER + SSD_GROUPS * D_STATE]
    c_all = xc[:, D_INNER + SSD_GROUPS * D_STATE:]

    dt = jax.nn.softplus(dt_ref[...] + dtb_ref[...])
    a = dt * -jnp.exp(alog_ref[...])
    a_row = jax.nn.softplus(dtt_ref[...] + dtbt_ref[...]) * -jnp.exp(alogt_ref[...])

    ri = lax.broadcasted_iota(I32, (CHUNK, CHUNK), 0)
    ci = lax.broadcasted_iota(I32, (CHUNK, CHUNK), 1)
    fwd = d == 0
    ahead = jnp.where(fwd, ri - ci, ci - ri)
    causal = ahead >= 0
    tri = jnp.where(causal, 1.0, 0.0).astype(BF16)
    tri_t = jnp.where(ahead <= 0, 1.0, 0.0).astype(BF16)
    cum = sum(_dot(tri, piece) for piece in _split3_bf16(a))
    cum_row = sum(_dot(piece, tri_t) for piece in _split3_bf16(a_row))
    total = jnp.sum(a, axis=0, keepdims=True)

    er = lax.broadcasted_iota(I32, (SSD_HEADS, D_INNER), 0)
    ec = lax.broadcasted_iota(I32, (SSD_HEADS, D_INNER), 1) // SSD_HEAD_DIM
    expand = jnp.where(er == ec, 1.0, 0.0).astype(BF16)
    stacked = jnp.concatenate([dt, jnp.exp(total - cum), jnp.exp(cum)], axis=0)
    stacked_e = sum(_dot(piece, expand) for piece in _split_bf16(stacked))
    dt_e = stacked_e[:CHUNK]
    dec_state_e = stacked_e[CHUNK:2 * CHUNK]
    dec_out_e = stacked_e[2 * CHUNK:]
    dec_chunk_e = sum(_dot(piece, expand) for piece in _split_bf16(jnp.exp(total)))

    xdt = xs * dt_e
    xdt_bf = xdt.astype(BF16)
    xdec_bf = (xdt * dec_state_e).astype(BF16)

    ys = []
    for g in range(SSD_GROUPS):
        bg = b_all[:, g * D_STATE:(g + 1) * D_STATE]
        cg = c_all[:, g * D_STATE:(g + 1) * D_STATE].astype(BF16)
        cb = _dot_nt(cg, bg.astype(BF16))
        lo = g * HEADS_PER_GROUP * SSD_HEAD_DIM
        hi = lo + HEADS_PER_GROUP * SSD_HEAD_DIM
        prev = state_ref[g]
        y_off = _dot(cg, prev.astype(BF16)) * dec_out_e[:, lo:hi]
        y_heads = []
        for r in range(HEADS_PER_GROUP):
            h = g * HEADS_PER_GROUP + r
            seg = cum[:, h:h + 1] - cum_row[h:h + 1, :]
            m = jnp.where(causal, cb * jnp.exp(jnp.where(causal, seg, 0.0)), 0.0)
            y_heads.append(_dot(m.astype(BF16), xdt_bf[:, h * SSD_HEAD_DIM:(h + 1) * SSD_HEAD_DIM]))
        ys.append(jnp.concatenate(y_heads, axis=-1) + y_off)
        st = _dot(bg.T.astype(BF16), xdec_bf[:, lo:hi])
        state_ref[g] = prev * dec_chunk_e[:, lo:hi] + st
    y = jnp.concatenate(ys, axis=-1)
    y_ref[...] = y + jnp.where(fwd, 1.0, 0.0) * (dskip_ref[...] * xs)


def _ssd_call(xbc, dt2, batch, seq, conv_w, conv_b, dt_bias, a_log, d_skip):
    tokens = batch * seq
    nc = seq // CHUNK
    halo_per_chunk = CHUNK // HALO
    n_halo = tokens // HALO
    dtt = dt2.reshape(2, batch, seq, SSD_HEADS).transpose(1, 0, 3, 2).reshape(batch * 2 * SSD_HEADS, seq)
    cw = jnp.zeros((8, CONV_CH), F32).at[:D_CONV].set(conv_w)
    cb = conv_b.reshape(1, CONV_CH)
    dtb = dt_bias.reshape(2, 1, SSD_HEADS)
    dtbt = dt_bias.reshape(2, SSD_HEADS, 1)
    na = a_log.reshape(2, 1, SSD_HEADS)
    nat = a_log.reshape(2, SSD_HEADS, 1)
    dskip_e = jnp.repeat(d_skip, SSD_HEAD_DIM).reshape(1, D_INNER)

    def ceff(d, c):
        return jnp.where(d == 0, c, nc - 1 - c)

    def main_idx(b, d, c):
        return (b * nc + ceff(d, c), 0)

    def prev_idx(b, d, c):
        return (jnp.maximum((b * nc + ceff(d, c)) * halo_per_chunk - 1, 0), 0)

    def next_idx(b, d, c):
        return (jnp.minimum((b * nc + ceff(d, c) + 1) * halo_per_chunk, n_halo - 1), 0)

    sel_d = lambda b, d, c: (d, 0, 0)
    fixed = lambda b, d, c: (0, 0)
    return pl.pallas_call(
        functools.partial(_ssd_kernel, n_chunks=nc),
        grid=(batch, 2, nc),
        in_specs=[
            pl.BlockSpec((CHUNK, CONV_CH), main_idx),
            pl.BlockSpec((HALO, CONV_CH), prev_idx),
            pl.BlockSpec((HALO, CONV_CH), next_idx),
            pl.BlockSpec((None, CHUNK, SSD_HEADS), lambda b, d, c: (d, b * nc + ceff(d, c), 0)),
            pl.BlockSpec((SSD_HEADS, CHUNK), lambda b, d, c: (b * 2 + d, ceff(d, c))),
            pl.BlockSpec((8, CONV_CH), fixed),
            pl.BlockSpec((1, CONV_CH), fixed),
            pl.BlockSpec((None, 1, SSD_HEADS), sel_d),
            pl.BlockSpec((None, SSD_HEADS, 1), sel_d),
            pl.BlockSpec((None, 1, SSD_HEADS), sel_d),
            pl.BlockSpec((None, SSD_HEADS, 1), sel_d),
            pl.BlockSpec((1, D_INNER), fixed),
        ],
        out_specs=pl.BlockSpec((None, CHUNK, D_INNER), lambda b, d, c: (d, b * nc + ceff(d, c), 0)),
        out_shape=jax.ShapeDtypeStruct((2, tokens, D_INNER), F32),
        scratch_shapes=[
            pltpu.VMEM((CHUNK + 2 * HALO, CONV_CH), F32),
            pltpu.VMEM((SSD_GROUPS, D_STATE, HEADS_PER_GROUP * SSD_HEAD_DIM), F32),
        ],
        compiler_params=_cparams(("parallel", "arbitrary", "arbitrary")),
        name="ssd",
    )(xbc, xbc, xbc, dt2, dtt, cw, cb, dtb, dtbt, na, nat, dskip_e)


def _mix_kernel(x_ref, ing_ref, inb_ref, attn_ref, yf_ref, yb_ref, z_ref, nw_ref, wo_ref,
                g_ref, b_ref, wrh_ref, wrl_ref, x1_ref, aff_ref):
    xln = _layer_norm_rows(x_ref[...], ing_ref[...], inb_ref[...])
    z = z_ref[...]
    y = (yf_ref[...] + yb_ref[...]) * (z * _sigmoid(z))
    gw = D_INNER // SSD_GROUPS
    parts = []
    for g in range(SSD_GROUPS):
        yg = y[:, g * gw:(g + 1) * gw]
        parts.append(yg * lax.rsqrt(jnp.mean(yg * yg, axis=-1, keepdims=True) + NORM_EPS))
    ssd = (jnp.concatenate(parts, axis=-1) * nw_ref[...]).astype(BF16)
    mix = _dot(jnp.concatenate([attn_ref[...], ssd], axis=-1), wo_ref[...])
    x1 = _layer_norm_rows(DN_ALPHA * xln + mix, g_ref[...], b_ref[...])
    x1_ref[...] = x1
    hi, lo = _split_bf16(x1)
    wrh = wrh_ref[...]
    logits = _dot(hi, wrh) + _dot(lo, wrh) + _dot(hi, wrl_ref[...])
    lane = lax.broadcasted_iota(I32, logits.shape, 1)
    logits = jnp.where(lane < N_EXPERTS, logits, -jnp.inf)
    e = jnp.exp(logits - jnp.max(logits, axis=-1, keepdims=True))
    aff = e / jnp.sum(e, axis=-1, keepdims=True)
    aff_ref[...] = aff[:, :N_EXPERTS]


def _mix_call(x2d, ln_in_g, ln_in_b, attn, y2, z, norm_w, w_out, ln_g, ln_b, wr_hi, wr_lo):
    tokens = x2d.shape[0]
    tm = TOKEN_TILE
    row = lambda i: (i, 0)
    fixed = lambda i: (0, 0)
    vec = pl.BlockSpec((1, D_MODEL), fixed)
    return pl.pallas_call(
        _mix_kernel,
        grid=(tokens // tm,),
        in_specs=[
            pl.BlockSpec((tm, D_MODEL), row), vec, vec,
            pl.BlockSpec((tm, ATTN_W), row),
            pl.BlockSpec((None, tm, D_INNER), lambda i: (0, i, 0)),
            pl.BlockSpec((None, tm, D_INNER), lambda i: (1, i, 0)),
            pl.BlockSpec((tm, D_INNER), row),
            pl.BlockSpec((1, D_INNER), fixed),
            pl.BlockSpec((D_MODEL, D_MODEL), fixed),
            vec, vec,
            pl.BlockSpec((D_MODEL, LANES), fixed),
            pl.BlockSpec((D_MODEL, LANES), fixed),
        ],
        out_specs=[
            pl.BlockSpec((tm, D_MODEL), row),
            pl.BlockSpec((tm, N_EXPERTS), row),
        ],
        out_shape=[
            jax.ShapeDtypeStruct((tokens, D_MODEL), F32),
            jax.ShapeDtypeStruct((tokens, N_EXPERTS), F32),
        ],
        compiler_params=_cparams(("parallel",)),
        name="mix",
    )(x2d, ln_in_g, ln_in_b, attn, y2, y2, z, norm_w, w_out, ln_g, ln_b, wr_hi, wr_lo)


def _tile_cumsum(mask_bf, upper, strict_lower):
    wloc = _dot(mask_bf, upper)
    tot = jnp.broadcast_to(wloc[:, LANES - 1:LANES], wloc.shape).astype(BF16)
    offs = _dot(strict_lower, tot)
    return wloc, offs


def _select_kernel(aff_ref, idx_ref, lp_ref, win_ref, *, cap):
    nt = aff_ref.shape[0]
    aff = aff_ref[...]

    t_bits = jnp.zeros((1, 1), I32)
    for bit in range(30, -1, -1):
        cand = t_bits | (1 << bit)
        cnt = jnp.sum(jnp.where(aff >= pltpu.bitcast(cand, F32), 1.0, 0.0), keepdims=True)
        t_bits = jnp.where(cnt >= cap, cand, t_bits)
    t = pltpu.bitcast(t_bits, F32)
    gt = aff > t
    eq = aff == t
    need = cap - jnp.sum(jnp.where(gt, 1.0, 0.0), keepdims=True)

    ri = lax.broadcasted_iota(I32, (LANES, LANES), 0)
    ci = lax.broadcasted_iota(I32, (LANES, LANES), 1)
    upper = jnp.where(ri <= ci, 1.0, 0.0).astype(BF16)
    rt = lax.broadcasted_iota(I32, (nt, nt), 0)
    ct = lax.broadcasted_iota(I32, (nt, nt), 1)
    strict_lower = jnp.where(ct < rt, 1.0, 0.0).astype(BF16)

    eq_loc, eq_offs = _tile_cumsum(jnp.where(eq, 1.0, 0.0).astype(BF16), upper, strict_lower)
    sel = gt | (eq & (eq_loc + eq_offs <= need))
    sel_bf = jnp.where(sel, 1.0, 0.0).astype(BF16)
    wloc, offs = _tile_cumsum(sel_bf, upper, strict_lower)

    sel_tiles = COMB_TILE // LANES
    first_lower = jnp.where(ct < (rt // sel_tiles) * sel_tiles, 1.0, 0.0).astype(BF16)
    offs_first = _dot(first_lower, jnp.broadcast_to(wloc[:, LANES - 1:LANES], wloc.shape).astype(BF16))
    win = jnp.minimum(jnp.floor(offs_first * (1.0 / BF16_SUBLANES)) * BF16_SUBLANES, float(cap - COMB_SPAN))
    win_ref[...] = win
    lp_ref[...] = jnp.where(sel, wloc + offs - 1.0 - win, -1.0)

    tot_row = _dot_nt(jnp.ones((8, LANES), BF16), sel_bf)
    upper_t = jnp.where(rt <= ct, 1.0, 0.0).astype(BF16)
    incl_row = _dot(tot_row.astype(BF16), upper_t)[0:1, :]
    excl_row = incl_row - tot_row[0:1, :]
    tile_id = lax.broadcasted_iota(I32, (nt, LANES), 0)
    lane_id = lax.broadcasted_iota(I32, (nt, LANES), 1)
    offs_hi = jnp.floor(offs * (1.0 / LANES))
    side = jnp.where(lane_id == 0, (tile_id // 2).astype(F32),
           jnp.where(lane_id == 1, (tile_id % 2).astype(F32),
           jnp.where(lane_id == 2, offs_hi,
           jnp.where(lane_id == 3, offs - offs_hi * LANES, 0.0))))
    table = jnp.concatenate([wloc, side], axis=-1).astype(BF16)
    for s0 in range(0, cap, SEL_SLOT_CHUNK):
        n = min(SEL_SLOT_CHUNK, cap - s0)
        slot = (lax.broadcasted_iota(I32, (n, 1), 0) + s0).astype(F32)
        onehot = jnp.where((excl_row <= slot) & (slot < incl_row), 1.0, 0.0).astype(BF16)
        got = _dot(onehot, table)
        tile = 2.0 * got[:, LANES:LANES + 1] + got[:, LANES + 1:LANES + 2]
        base = LANES * got[:, LANES + 2:LANES + 3] + got[:, LANES + 3:LANES + 4]
        lane = jnp.sum(jnp.where(got[:, :LANES] <= slot - base, 1.0, 0.0), axis=-1, keepdims=True)
        tok = jnp.broadcast_to(tile * LANES + lane, (n, LANES))
        idx_ref[:, s0:s0 + n] = tok.T[0:1, :].astype(I32)


def _select_call(aff):
    tokens = aff.shape[0]
    cap = CAPACITY_FACTOR * tokens // N_EXPERTS
    nt = tokens // LANES
    aff_t = aff.T.reshape(N_EXPERTS, nt, LANES)
    blk = pl.BlockSpec((None, nt, LANES), lambda e: (e, 0, 0))
    idx, lp, win = pl.pallas_call(
        functools.partial(_select_kernel, cap=cap),
        grid=(N_EXPERTS,),
        in_specs=[blk],
        out_specs=[pl.BlockSpec((None, 1, cap), lambda e: (e, 0, 0)), blk, blk],
        out_shape=[
            jax.ShapeDtypeStruct((N_EXPERTS, 1, cap), I32),
            jax.ShapeDtypeStruct((N_EXPERTS, nt, LANES), F32),
            jax.ShapeDtypeStruct((N_EXPERTS, nt, LANES), F32),
        ],
        compiler_params=_cparams(("parallel",)),
        name="select",
    )(aff_t)
    lp_tok = lp.reshape(N_EXPERTS, tokens).T
    win_start = win[:, ::COMB_TILE // LANES, 0].astype(I32).reshape(-1)
    return idx.reshape(N_EXPERTS * cap), lp_tok, win_start, cap


def _ffn_kernel(idx_ref, x_hbm, wg_ref, wu_ref, wd_ref, ye_ref, land_a, land_b, xe_a, xe_b, sem_ref,
                *, cap):
    e = pl.program_id(0)
    c = pl.program_id(1)
    pairs = cap // (2 * FFN_ROWS)
    step = e * pairs + c
    n_steps = N_EXPERTS * pairs
    last = 2 * n_steps - 1
    land = (land_a, land_b)
    stage = (xe_a, xe_b)

    def row_copy(chunk, r, half):
        tok = idx_ref[chunk * FFN_ROWS + r]
        return pltpu.make_async_copy(x_hbm.at[pl.ds(tok, 1), :], land[half].at[pl.ds(r, 1), :],
                                     sem_ref.at[half])

    def start_chunk(chunk, half):
        for r in range(FFN_ROWS):
            row_copy(chunk, r, half).start()

    def wait_chunk(chunk, half):
        def body(r, carry):
            row_copy(chunk, r, half).wait()
            return carry
        lax.fori_loop(0, FFN_ROWS, body, 0, unroll=8)

    @pl.when(step == 0)
    def _():
        start_chunk(0, 0)
        start_chunk(1, 1)

    for half in range(2):
        wait_chunk(2 * step + half, half)
        stage[half][...] = land[half][...].astype(BF16)
        start_chunk(jnp.minimum(2 * step + 2 + half, last), half)
        xe = stage[half][...]
        gate = _dot(xe, wg_ref[...])
        up = _dot(xe, wu_ref[...])
        hid = (gate * _sigmoid(gate) * up).astype(BF16)
        ye_ref[half * FFN_ROWS:(half + 1) * FFN_ROWS, :] = _dot(hid, wd_ref[...]).astype(BF16)

    @pl.when(step == n_steps - 1)
    def _():
        wait_chunk(last, 0)
        wait_chunk(last, 1)


def _ffn_call(idx, x1, wg, wu, wd, cap):
    pairs = cap // (2 * FFN_ROWS)
    wspec = lambda d0, d1: pl.BlockSpec((None, d0, d1), lambda e, c, idx: (e, 0, 0))
    grid_spec = pltpu.PrefetchScalarGridSpec(
        num_scalar_prefetch=1,
        grid=(N_EXPERTS, pairs),
        in_specs=[
            pl.BlockSpec(memory_space=pl.ANY),
            wspec(D_MODEL, D_EXPERT), wspec(D_MODEL, D_EXPERT), wspec(D_EXPERT, D_MODEL),
        ],
        out_specs=pl.BlockSpec((None, 2 * FFN_ROWS, D_MODEL), lambda e, c, idx: (e, c, 0)),
        scratch_shapes=[
            pltpu.VMEM((FFN_ROWS, D_MODEL), F32),
            pltpu.VMEM((FFN_ROWS, D_MODEL), F32),
            pltpu.VMEM((FFN_ROWS, D_MODEL), BF16),
            pltpu.VMEM((FFN_ROWS, D_MODEL), BF16),
            pltpu.SemaphoreType.DMA((2,)),
        ],
    )
    return pl.pallas_call(
        functools.partial(_ffn_kernel, cap=cap),
        grid_spec=grid_spec,
        out_shape=jax.ShapeDtypeStruct((N_EXPERTS, cap, D_MODEL), BF16),
        compiler_params=_cparams(("arbitrary", "arbitrary")),
        name="ffn",
    )(idx, x1, wg, wu, wd)


def _combine_kernel(win_ref, x1_ref, p_ref, lp_ref, aff_ref, ye_hbm, wpg_ref, bpg_ref, wple_ref,
                    g_ref, b_ref, o_ref, buf_ref, xbuf_ref, acc_ref, sem_ref, xsem_ref, *, n_tiles):
    i = pl.program_id(0)
    slot = i % 2
    width = N_EXPERTS * COMB_WIN

    def win_copy(tile, e, rnd, dst, sem):
        start = pl.multiple_of(win_ref[e * n_tiles + tile] + rnd * COMB_WIN, BF16_SUBLANES)
        return pltpu.make_async_copy(ye_hbm.at[e, pl.ds(start, COMB_WIN), :],
                                     dst.at[pl.ds(e * COMB_WIN, COMB_WIN), :], sem)

    def start_tile(tile, buf_slot):
        for e in range(N_EXPERTS):
            win_copy(tile, e, 0, buf_ref.at[buf_slot], sem_ref.at[buf_slot]).start()

    @pl.when(i == 0)
    def _():
        start_tile(0, 0)

    @pl.when(i + 1 < n_tiles)
    def _():
        start_tile(i + 1, 1 - slot)

    x1 = x1_ref[...]
    ple = _sigmoid(_dot(x1.astype(BF16), wpg_ref[...]) + bpg_ref[...]) \
        * _dot(p_ref[...].astype(BF16), wple_ref[...])
    acc_ref[...] = DN_ALPHA * x1 + ple

    lp = lp_ref[...]
    rnd_of = jnp.floor(lp * (1.0 / COMB_WIN))
    row_of = lp - rnd_of * COMB_WIN
    er = lax.broadcasted_iota(I32, (N_EXPERTS, width), 0)
    ec = lax.broadcasted_iota(I32, (N_EXPERTS, width), 1) // COMB_WIN
    expand = jnp.where(er == ec, 1.0, 0.0).astype(BF16)
    lane_row = (lax.broadcasted_iota(I32, (COMB_TILE, width), 1) % COMB_WIN).astype(F32)
    hit = _dot(row_of.astype(BF16), expand) == lane_row
    rnd_e = _dot(rnd_of.astype(BF16), expand)
    g_hi, g_lo = _split_bf16(aff_ref[...])
    g_hi_e = jnp.where(hit, _dot(g_hi, expand), 0.0)
    g_lo_e = jnp.where(hit, _dot(g_lo, expand), 0.0)

    def add_round(rnd, rows):
        pick = rnd_e == rnd
        lhs = jnp.concatenate([jnp.where(pick, g_hi_e, 0.0).astype(BF16),
                               jnp.where(pick, g_lo_e, 0.0).astype(BF16)], axis=0)
        both = _dot(lhs, rows)
        acc_ref[...] += both[:COMB_TILE] + both[COMB_TILE:]

    for e in range(N_EXPERTS):
        win_copy(i, e, 0, buf_ref.at[slot], sem_ref.at[slot]).wait()
    add_round(0.0, buf_ref[slot])

    n_rounds = jnp.max(rnd_of).astype(I32) + 1
    for rnd in range(1, COMB_ROUNDS):
        @pl.when(rnd < n_rounds)
        def _():
            for e in range(N_EXPERTS):
                win_copy(i, e, rnd, xbuf_ref, xsem_ref.at[0]).start()
            for e in range(N_EXPERTS):
                win_copy(i, e, rnd, xbuf_ref, xsem_ref.at[0]).wait()
            add_round(float(rnd), xbuf_ref[...])

    o_ref[...] = _layer_norm_rows(acc_ref[...], g_ref[...], b_ref[...])


def _combine_call(win_start, x1, p2d, lp_tok, aff, ye, wpg, bpg, wple, ln_g, ln_b):
    tokens = x1.shape[0]
    n_tiles = tokens // COMB_TILE
    row = lambda i, w: (i, 0)
    fixed = lambda i, w: (0, 0)
    vec = pl.BlockSpec((1, D_MODEL), fixed)
    grid_spec = pltpu.PrefetchScalarGridSpec(
        num_scalar_prefetch=1,
        grid=(n_tiles,),
        in_specs=[
            pl.BlockSpec((COMB_TILE, D_MODEL), row),
            pl.BlockSpec((COMB_TILE, PLE_DIM), row),
            pl.BlockSpec((COMB_TILE, N_EXPERTS), row),
            pl.BlockSpec((COMB_TILE, N_EXPERTS), row),
            pl.BlockSpec(memory_space=pl.ANY),
            pl.BlockSpec((D_MODEL, D_MODEL), fixed),
            vec,
            pl.BlockSpec((PLE_DIM, D_MODEL), fixed),
            vec, vec,
        ],
        out_specs=pl.BlockSpec((COMB_TILE, D_MODEL), row),
        scratch_shapes=[
            pltpu.VMEM((2, N_EXPERTS * COMB_WIN, D_MODEL), BF16),
            pltpu.VMEM((N_EXPERTS * COMB_WIN, D_MODEL), BF16),
            pltpu.VMEM((COMB_TILE, D_MODEL), F32),
            pltpu.SemaphoreType.DMA((2,)),
            pltpu.SemaphoreType.DMA((1,)),
        ],
    )
    return pl.pallas_call(
        functools.partial(_combine_kernel, n_tiles=n_tiles),
        grid_spec=grid_spec,
        out_shape=jax.ShapeDtypeStruct((tokens, D_MODEL), F32),
        compiler_params=_cparams(("arbitrary",)),
        name="combine",
    )(win_start, x1, p2d, lp_tok, aff, ye, wpg, bpg, wple, ln_g, ln_b)


def _prep_weights(ln_in_g, ln_in_b, w_in, q_norm_w, k_norm_w, conv_w, conv_b, dt_bias, a_log,
                  d_skip, ssd_norm_w, w_out, ln1_g, ln1_b, w_router, w_e_gate, w_e_up, w_e_down,
                  w_ple_gate, b_ple_gate, w_ple, ln2_g, ln2_b):
    l = 0
    vec = lambda a: a.reshape(1, -1)
    w_in_pad = jnp.zeros((D_MODEL, IN_COLS_PAD), F32).at[:, :IN_COLS].set(w_in[l]).astype(BF16)
    wr = jnp.zeros((D_MODEL, LANES), F32).at[:, :N_EXPERTS].set(w_router[l])
    wr_hi, wr_lo = _split_bf16(wr)
    return dict(
        ln_in_g=vec(ln_in_g), ln_in_b=vec(ln_in_b), w_in=w_in_pad,
        qw=vec(jnp.tile(q_norm_w[l], N_Q_HEADS)), kw=vec(jnp.tile(k_norm_w[l], N_KV_HEADS)),
        conv_w=conv_w[l], conv_b=conv_b[l], dt_bias=dt_bias[l], a_log=a_log[l], d_skip=d_skip[l],
        norm_w=vec(ssd_norm_w[l]), w_out=w_out[l].astype(BF16), ln1_g=vec(ln1_g[l]), ln1_b=vec(ln1_b[l]),
        wr_hi=wr_hi, wr_lo=wr_lo,
        wg=w_e_gate[l].astype(BF16), wu=w_e_up[l].astype(BF16), wd=w_e_down[l].astype(BF16),
        wpg=w_ple_gate[l].astype(BF16), bpg=vec(b_ple_gate[l]), wple=w_ple[l].astype(BF16),
        ln2_g=vec(ln2_g[l]), ln2_b=vec(ln2_b[l]),
    )


def _trunk(x, p, w):
    batch, seq, _ = x.shape
    tokens = batch * seq
    x2d = x.reshape(tokens, D_MODEL)
    q, k, v, z, xbc, dt2 = _inproj_call(x2d, seq, w["ln_in_g"], w["ln_in_b"], w["w_in"], w["qw"], w["kw"])
    attn = _attn_call(q, k, v, batch, seq)
    y2 = _ssd_call(xbc, dt2, batch, seq, w["conv_w"], w["conv_b"], w["dt_bias"], w["a_log"], w["d_skip"])
    x1, aff = _mix_call(x2d, w["ln_in_g"], w["ln_in_b"], attn, y2, z, w["norm_w"], w["w_out"],
                        w["ln1_g"], w["ln1_b"], w["wr_hi"], w["wr_lo"])
    idx, lp_tok, win_start, cap = _select_call(aff)
    ye = _ffn_call(idx, x1, w["wg"], w["wu"], w["wd"], cap)
    out = _combine_call(win_start, x1, p[0].reshape(tokens, PLE_DIM), lp_tok, aff, ye,
                        w["wpg"], w["bpg"], w["wple"], w["ln2_g"], w["ln2_b"])
    return out.reshape(batch, seq, D_MODEL)


def kernel(x_prompt, x_sample, p_prompt, p_sample, ln_in_g, ln_in_b, w_in, q_norm_w, k_norm_w, conv_w, conv_b, dt_bias, a_log, d_skip, ssd_norm_w, w_out, ln1_g, ln1_b, w_router, w_e_gate, w_e_up, w_e_down, w_ple_gate, b_ple_gate, w_ple, ln2_g, ln2_b):
    w = _prep_weights(ln_in_g, ln_in_b, w_in, q_norm_w, k_norm_w, conv_w, conv_b, dt_bias, a_log,
                      d_skip, ssd_norm_w, w_out, ln1_g, ln1_b, w_router, w_e_gate, w_e_up, w_e_down,
                      w_ple_gate, b_ple_gate, w_ple, ln2_g, ln2_b)
    return (_trunk(x_prompt, p_prompt, w), _trunk(x_sample, p_sample, w))
```

```python
import functools
import math

import jax
import jax.numpy as jnp
from jax import lax
from jax.experimental import pallas as pl
from jax.experimental.pallas import tpu as pltpu

F32 = jnp.float32
BF16 = jnp.bfloat16
I32 = jnp.int32

D_MODEL = 1024
DEPTH = 1
GRID_W = 64
N_Q_HEADS = 8
N_KV_HEADS = 2
HEAD_DIM = 64
Q_PER_KV = N_Q_HEADS // N_KV_HEADS
ATTN_W = N_Q_HEADS * HEAD_DIM
KV_W = N_KV_HEADS * HEAD_DIM
ROPE_THETA = 10000.0
SSD_HEADS = 8
SSD_HEAD_DIM = 64
D_INNER = SSD_HEADS * SSD_HEAD_DIM
SSD_GROUPS = 2
HEADS_PER_GROUP = SSD_HEADS // SSD_GROUPS
D_STATE = 128
D_CONV = 5
CHUNK = 128
CONV_CH = D_INNER + 2 * SSD_GROUPS * D_STATE
K0 = ATTN_W
V0 = K0 + KV_W
Z0 = V0 + KV_W
XBC0 = Z0 + D_INNER
DT0 = XBC0 + CONV_CH
IN_COLS = DT0 + 2 * SSD_HEADS
N_EXPERTS = 16
CAPACITY_FACTOR = 2
D_EXPERT = 1024
PLE_DIM = 256
DN_ALPHA = (2.0 * DEPTH) ** 0.25
NORM_EPS = 1e-6
LN_EPS = 1e-5

LANES = 128
BF16_SUBLANES = 16
IN_COLS_PAD = DT0 + LANES
TOKEN_TILE = 512
ROW_GROUPS = 2
ATTN_Q_TILE = 256
FFN_ROWS = 256
COMB_TILE = 256
COMB_WIN = 64
COMB_ROUNDS = -(-(COMB_TILE + BF16_SUBLANES - 1) // COMB_WIN)
COMB_SPAN = COMB_ROUNDS * COMB_WIN
SEL_SLOT_CHUNK = 1024
VMEM_LIMIT = 56 * 1024 * 1024


def _cparams(sem):
    return pltpu.CompilerParams(dimension_semantics=sem, vmem_limit_bytes=VMEM_LIMIT)


def _layer_norm_rows(x, g, b):
    xc = x - jnp.mean(x, axis=-1, keepdims=True)
    return xc * lax.rsqrt(jnp.mean(xc * xc, axis=-1, keepdims=True) + LN_EPS) * g + b


def _sigmoid(x):
    return 1.0 / (1.0 + jnp.exp(-x))


def _dot(a, b):
    return jnp.dot(a, b, preferred_element_type=F32)


def _dot_nt(a, b):
    return lax.dot_general(a, b, (((1,), (1,)), ((), ())), preferred_element_type=F32)


def _split_bf16(x):
    hi = x.astype(BF16)
    lo = (x - hi.astype(F32)).astype(BF16)
    return hi, lo


def _split3_bf16(x):
    hi = x.astype(BF16)
    rest = x - hi.astype(F32)
    mid = rest.astype(BF16)
    lo = (rest - mid.astype(F32)).astype(BF16)
    return hi, mid, lo


def _rope_swap(x):
    outs = []
    for c in range(x.shape[-1] // LANES):
        xc = x[:, c * LANES:(c + 1) * LANES]
        lane = lax.broadcasted_iota(I32, xc.shape, 1)
        up = pltpu.roll(xc, LANES - 16, axis=1)
        dn = pltpu.roll(xc, 16, axis=1)
        outs.append(jnp.where((lane % 32) < 16, up, dn))
    return outs[0] if len(outs) == 1 else jnp.concatenate(outs, axis=-1)


def _head_rms(x, w):
    width = x.shape[-1]
    r = lax.broadcasted_iota(I32, (width, width), 0) // HEAD_DIM
    c = lax.broadcasted_iota(I32, (width, width), 1) // HEAD_DIM
    blockdiag = jnp.where(r == c, 1.0, 0.0).astype(BF16)
    hi, lo = _split_bf16(x * x)
    ssq = _dot(hi, blockdiag) + _dot(lo, blockdiag)
    return x * lax.rsqrt(ssq * (1.0 / HEAD_DIM) + NORM_EPS) * w


def _inproj_kernel(x_ref, g_ref, b_ref, w_ref, qw_ref, kw_ref, cos_ref, sin_ref,
                   q_ref, k_ref, v_ref, z_ref, xbc_ref, dt_ref):
    sub = x_ref.shape[0] // ROW_GROUPS
    for grp in range(ROW_GROUPS):
        rs = slice(grp * sub, (grp + 1) * sub)
        xln = _layer_norm_rows(x_ref[rs, :], g_ref[...], b_ref[...])
        proj = _dot(xln.astype(BF16), w_ref[...])
        cos = cos_ref[rs, :]
        sin = sin_ref[rs, :]
        q = _head_rms(proj[:, :K0], qw_ref[...])
        cos_q = jnp.concatenate([cos] * (ATTN_W // LANES), axis=-1)
        sin_q = jnp.concatenate([sin] * (ATTN_W // LANES), axis=-1)
        q = (q * cos_q + _rope_swap(q) * sin_q) * (HEAD_DIM ** -0.5)
        q_ref[rs, :] = q.astype(BF16)
        k = _head_rms(proj[:, K0:V0], kw_ref[...])
        k = (k * cos + _rope_swap(k) * sin).astype(BF16)
        v = proj[:, V0:Z0]
        ones_col = jnp.where(lax.broadcasted_iota(I32, (sub, LANES - HEAD_DIM), 1) == 0, 1.0, 0.0)
        for g in range(N_KV_HEADS):
            k_ref[g, rs, :] = k[:, g * HEAD_DIM:(g + 1) * HEAD_DIM]
            v_ref[g, rs, :] = jnp.concatenate([v[:, g * HEAD_DIM:(g + 1) * HEAD_DIM], ones_col],
                                              axis=-1).astype(BF16)
        z_ref[rs, :] = proj[:, Z0:XBC0]
        xbc_ref[rs, :] = proj[:, XBC0:DT0]
        dt = proj[:, DT0:DT0 + LANES]
        for d in range(2):
            dt_ref[d, rs, :] = dt[:, d * SSD_HEADS:(d + 1) * SSD_HEADS]


def _rope_tables(seq):
    rows = seq // GRID_W
    row_id = jnp.broadcast_to(jnp.arange(rows)[:, None], (rows, GRID_W)).reshape(seq)
    col_id = jnp.broadcast_to(jnp.arange(GRID_W)[None, :], (rows, GRID_W)).reshape(seq)
    pos = jnp.stack([row_id, col_id], axis=-1).astype(F32)
    n_freq = HEAD_DIM // 4
    inv_freq = ROPE_THETA ** (-jnp.arange(n_freq, dtype=F32) / n_freq)
    ang = pos[:, :, None] * inv_freq
    cos = jnp.broadcast_to(jnp.cos(ang)[:, :, None, :], (seq, 2, 2, n_freq)).reshape(seq, HEAD_DIM)
    sin = jnp.broadcast_to(jnp.sin(ang)[:, :, None, :], (seq, 2, 2, n_freq))
    sin = (sin * jnp.array([-1.0, 1.0], F32)[None, None, :, None]).reshape(seq, HEAD_DIM)
    reps = LANES // HEAD_DIM
    return jnp.tile(cos, (1, reps)), jnp.tile(sin, (1, reps))


def _inproj_call(x2d, seq, ln_g, ln_b, w_in_pad, qw, kw):
    tokens = x2d.shape[0]
    tm = TOKEN_TILE
    nt = tokens // tm
    tiles_per_seq = seq // tm
    cos, sin = _rope_tables(seq)
    row = lambda i: (i, 0)
    fixed = lambda i: (0, 0)
    return pl.pallas_call(
        _inproj_kernel,
        grid=(nt,),
        in_specs=[
            pl.BlockSpec((tm, D_MODEL), row),
            pl.BlockSpec((1, D_MODEL), fixed),
            pl.BlockSpec((1, D_MODEL), fixed),
            pl.BlockSpec((D_MODEL, IN_COLS_PAD), fixed),
            pl.BlockSpec((1, ATTN_W), fixed),
            pl.BlockSpec((1, KV_W), fixed),
            pl.BlockSpec((tm, LANES), lambda i: (i % tiles_per_seq, 0)),
            pl.BlockSpec((tm, LANES), lambda i: (i % tiles_per_seq, 0)),
        ],
        out_specs=[
            pl.BlockSpec((tm, ATTN_W), row),
            pl.BlockSpec((N_KV_HEADS, tm, HEAD_DIM), lambda i: (0, i, 0)),
            pl.BlockSpec((N_KV_HEADS, tm, LANES), lambda i: (0, i, 0)),
            pl.BlockSpec((tm, D_INNER), row),
            pl.BlockSpec((tm, CONV_CH), row),
            pl.BlockSpec((2, tm, SSD_HEADS), lambda i: (0, i, 0)),
        ],
        out_shape=[
            jax.ShapeDtypeStruct((tokens, ATTN_W), BF16),
            jax.ShapeDtypeStruct((N_KV_HEADS, tokens, HEAD_DIM), BF16),
            jax.ShapeDtypeStruct((N_KV_HEADS, tokens, LANES), BF16),
            jax.ShapeDtypeStruct((tokens, D_INNER), F32),
            jax.ShapeDtypeStruct((tokens, CONV_CH), F32),
            jax.ShapeDtypeStruct((2, tokens, SSD_HEADS), F32),
        ],
        compiler_params=_cparams(("parallel",)),
        name="inproj",
    )(x2d, ln_g, ln_b, w_in_pad, qw, kw, cos, sin)


def _attn_kernel(q_ref, k_ref, v_ref, o_ref):
    k = k_ref[...]
    v = v_ref[...]
    q = q_ref[...]
    outs = []
    for h in range(Q_PER_KV):
        s = _dot_nt(q[:, h * HEAD_DIM:(h + 1) * HEAD_DIM], k)
        m = jnp.max(s, axis=-1, keepdims=True)
        p = jnp.exp((s - m).astype(BF16))
        o = _dot(p, v)
        outs.append(o[:, :HEAD_DIM] / o[:, HEAD_DIM:HEAD_DIM + 1])
    o_ref[...] = jnp.concatenate(outs, axis=-1).astype(BF16)


def _attn_call(q, k, v, batch, seq):
    tokens = batch * seq
    tq = ATTN_Q_TILE
    nq = seq // tq
    width = Q_PER_KV * HEAD_DIM
    return pl.pallas_call(
        _attn_kernel,
        grid=(batch, N_KV_HEADS, nq),
        in_specs=[
            pl.BlockSpec((tq, width), lambda b, g, i: (b * nq + i, g)),
            pl.BlockSpec((None, seq, HEAD_DIM), lambda b, g, i: (g, b, 0)),
            pl.BlockSpec((None, seq, LANES), lambda b, g, i: (g, b, 0)),
        ],
        out_specs=pl.BlockSpec((tq, width), lambda b, g, i: (b * nq + i, g)),
        out_shape=jax.ShapeDtypeStruct((tokens, ATTN_W), BF16),
        compiler_params=_cparams(("parallel", "parallel", "parallel")),
        name="attn",
    )(q, k, v)


HALO = 8
SSD_BLOCK = 4


def _ssd_kernel(x_ref, prev_ref, next_ref, dt_ref, dtt_ref, cw_ref, cb_ref, dtb_ref, dtbt_ref,
                alog_ref, alogt_ref, dskip_ref, y_ref, xc_ref, state_ref, *, n_blocks):
    d = pl.program_id(1)
    c = pl.program_id(2)
    fwd = d == 0
    blk = jnp.where(fwd, c, n_blocks - 1 - c)
    rows = SSD_BLOCK * CHUNK

    @pl.when(c == 0)
    def _():
        state_ref[...] = jnp.zeros_like(state_ref)

    has_prev = (blk > 0).astype(F32)
    has_next = (blk < n_blocks - 1).astype(F32)
    padded = jnp.concatenate([prev_ref[...] * has_prev, x_ref[...], next_ref[...] * has_next], axis=0)
    cw = cw_ref[...]
    acc = jnp.broadcast_to(cb_ref[...], (rows, CONV_CH))
    for tap in range(D_CONV):
        shift = (D_CONV // 2 - tap) % (rows + 2 * HALO)
        shifted = padded if shift == 0 else pltpu.roll(padded, shift, axis=0)
        acc = acc + shifted[HALO:HALO + rows, :] * cw[tap:tap + 1, :]
    xc_ref[...] = acc * _sigmoid(acc)

    for j in range(SSD_BLOCK):
        sub = jnp.where(fwd, j, SSD_BLOCK - 1 - j)
        _ssd_chunk(pl.multiple_of(sub * CHUNK, CHUNK), fwd, xc_ref, dt_ref, dtt_ref, dtb_ref, dtbt_ref,
                   alog_ref, alogt_ref, dskip_ref, y_ref, state_ref)


def _ssd_chunk(row0, fwd, xc_ref, dt_ref, dtt_ref, dtb_ref, dtbt_ref, alog_ref, alogt_ref, dskip_ref,
               y_ref, state_ref):
    rows = pl.ds(row0, CHUNK)
    xc = xc_ref[rows, :]
    xs = xc[:, :D_INNER]
    b_all = xc[:, D_INNER:D_INNER + SSD_GROUPS * D_STATE]
    c_all = xc[:, D_INNER + SSD_GROUPS * D_STATE:]

    dt = jax.nn.softplus(dt_ref[rows, :] + dtb_ref[...])
    a = dt * -jnp.exp(alog_ref[...])
    a_row = jax.nn.softplus(dtt_ref[:, rows] + dtbt_ref[...]) * -jnp.exp(alogt_ref[...])

    ri = lax.broadcasted_iota(I32, (CHUNK, CHUNK), 0)
    ci = lax.broadcasted_iota(I32, (CHUNK, CHUNK), 1)
    ahead = jnp.where(fwd, ri - ci, ci - ri)
    causal = ahead >= 0
    tri = jnp.where(causal, 1.0, 0.0).astype(BF16)
    tri_t = jnp.where(ahead <= 0, 1.0, 0.0).astype(BF16)
    cum = sum(_dot(tri, piece) for piece in _split3_bf16(a))
    cum_row = sum(_dot(piece, tri_t) for piece in _split3_bf16(a_row))
    total = jnp.sum(a, axis=0, keepdims=True)

    er = lax.broadcasted_iota(I32, (SSD_HEADS, D_INNER), 0)
    ec = lax.broadcasted_iota(I32, (SSD_HEADS, D_INNER), 1) // SSD_HEAD_DIM
    expand = jnp.where(er == ec, 1.0, 0.0).astype(BF16)
    stacked = jnp.concatenate([dt, jnp.exp(total - cum), jnp.exp(cum)], axis=0)
    stacked_e = sum(_dot(piece, expand) for piece in _split_bf16(stacked))
    dt_e = stacked_e[:CHUNK]
    dec_state_e = stacked_e[CHUNK:2 * CHUNK]
    dec_out_e = stacked_e[2 * CHUNK:]
    dec_chunk_e = sum(_dot(piece, expand) for piece in _split_bf16(jnp.exp(total)))

    xdt = xs * dt_e
    xdt_bf = xdt.astype(BF16)
    xdec_bf = (xdt * dec_state_e).astype(BF16)

    ys = []
    for g in range(SSD_GROUPS):
        bg = b_all[:, g * D_STATE:(g + 1) * D_STATE]
        cg = c_all[:, g * D_STATE:(g + 1) * D_STATE].astype(BF16)
        cb = _dot_nt(cg, bg.astype(BF16))
        lo = g * HEADS_PER_GROUP * SSD_HEAD_DIM
        hi = lo + HEADS_PER_GROUP * SSD_HEAD_DIM
        prev = state_ref[g]
        y_off = _dot(cg, prev.astype(BF16)) * dec_out_e[:, lo:hi]
        y_heads = []
        for r in range(HEADS_PER_GROUP):
            h = g * HEADS_PER_GROUP + r
            seg = cum[:, h:h + 1] - cum_row[h:h + 1, :]
            m = jnp.where(causal, cb * jnp.exp(jnp.where(causal, seg, 0.0)), 0.0)
            y_heads.append(_dot(m.astype(BF16), xdt_bf[:, h * SSD_HEAD_DIM:(h + 1) * SSD_HEAD_DIM]))
        ys.append(jnp.concatenate(y_heads, axis=-1) + y_off)
        st = _dot(bg.T.astype(BF16), xdec_bf[:, lo:hi])
        state_ref[g] = prev * dec_chunk_e[:, lo:hi] + st
    y = jnp.concatenate(ys, axis=-1)
    y_ref[rows, :] = y + jnp.where(fwd, 1.0, 0.0) * (dskip_ref[...] * xs)


def _ssd_call(xbc, dt2, batch, seq, conv_w, conv_b, dt_bias, a_log, d_skip):
    tokens = batch * seq
    rows = SSD_BLOCK * CHUNK
    nc = seq // rows
    halo_per_chunk = rows // HALO
    n_halo = tokens // HALO
    dtt = dt2.reshape(2, batch, seq, SSD_HEADS).transpose(1, 0, 3, 2).reshape(batch * 2 * SSD_HEADS, seq)
    cw = jnp.zeros((8, CONV_CH), F32).at[:D_CONV].set(conv_w)
    cb = conv_b.reshape(1, CONV_CH)
    dtb = dt_bias.reshape(2, 1, SSD_HEADS)
    dtbt = dt_bias.reshape(2, SSD_HEADS, 1)
    na = a_log.reshape(2, 1, SSD_HEADS)
    nat = a_log.reshape(2, SSD_HEADS, 1)
    dskip_e = jnp.repeat(d_skip, SSD_HEAD_DIM).reshape(1, D_INNER)

    def ceff(d, c):
        return jnp.where(d == 0, c, nc - 1 - c)

    def main_idx(b, d, c):
        return (b * nc + ceff(d, c), 0)

    def prev_idx(b, d, c):
        return (jnp.maximum((b * nc + ceff(d, c)) * halo_per_chunk - 1, 0), 0)

    def next_idx(b, d, c):
        return (jnp.minimum((b * nc + ceff(d, c) + 1) * halo_per_chunk, n_halo - 1), 0)

    sel_d = lambda b, d, c: (d, 0, 0)
    fixed = lambda b, d, c: (0, 0)
    return pl.pallas_call(
        functools.partial(_ssd_kernel, n_blocks=nc),
        grid=(batch, 2, nc),
        in_specs=[
            pl.BlockSpec((rows, CONV_CH), main_idx),
            pl.BlockSpec((HALO, CONV_CH), prev_idx),
            pl.BlockSpec((HALO, CONV_CH), next_idx),
            pl.BlockSpec((None, rows, SSD_HEADS), lambda b, d, c: (d, b * nc + ceff(d, c), 0)),
            pl.BlockSpec((SSD_HEADS, rows), lambda b, d, c: (b * 2 + d, ceff(d, c))),
            pl.BlockSpec((8, CONV_CH), fixed),
            pl.BlockSpec((1, CONV_CH), fixed),
            pl.BlockSpec((None, 1, SSD_HEADS), sel_d),
            pl.BlockSpec((None, SSD_HEADS, 1), sel_d),
            pl.BlockSpec((None, 1, SSD_HEADS), sel_d),
            pl.BlockSpec((None, SSD_HEADS, 1), sel_d),
            pl.BlockSpec((1, D_INNER), fixed),
        ],
        out_specs=pl.BlockSpec((None, rows, D_INNER), lambda b, d, c: (d, b * nc + ceff(d, c), 0)),
        out_shape=jax.ShapeDtypeStruct((2, tokens, D_INNER), F32),
        scratch_shapes=[
            pltpu.VMEM((rows, CONV_CH), F32),
            pltpu.VMEM((SSD_GROUPS, D_STATE, HEADS_PER_GROUP * SSD_HEAD_DIM), F32),
        ],
        compiler_params=_cparams(("parallel", "arbitrary", "arbitrary")),
        name="ssd",
    )(xbc, xbc, xbc, dt2, dtt, cw, cb, dtb, dtbt, na, nat, dskip_e)


def _mix_kernel(x_ref, ing_ref, inb_ref, attn_ref, yf_ref, yb_ref, z_ref, nw_ref, wo_ref,
                g_ref, b_ref, wrh_ref, wrl_ref, x1_ref, aff_ref):
    sub = x_ref.shape[0] // ROW_GROUPS
    for grp in range(ROW_GROUPS):
        rs = slice(grp * sub, (grp + 1) * sub)
        xln = _layer_norm_rows(x_ref[rs, :], ing_ref[...], inb_ref[...])
        z = z_ref[rs, :]
        y = (yf_ref[rs, :] + yb_ref[rs, :]) * (z * _sigmoid(z))
        gw = D_INNER // SSD_GROUPS
        parts = []
        for g in range(SSD_GROUPS):
            yg = y[:, g * gw:(g + 1) * gw]
            parts.append(yg * lax.rsqrt(jnp.mean(yg * yg, axis=-1, keepdims=True) + NORM_EPS))
        ssd = (jnp.concatenate(parts, axis=-1) * nw_ref[...]).astype(BF16)
        mix = _dot(jnp.concatenate([attn_ref[rs, :], ssd], axis=-1), wo_ref[...])
        x1 = _layer_norm_rows(DN_ALPHA * xln + mix, g_ref[...], b_ref[...])
        x1_ref[rs, :] = x1
        hi, lo = _split_bf16(x1)
        wrh = wrh_ref[...]
        logits = _dot(hi, wrh) + _dot(lo, wrh) + _dot(hi, wrl_ref[...])
        lane = lax.broadcasted_iota(I32, logits.shape, 1)
        logits = jnp.where(lane < N_EXPERTS, logits, -jnp.inf)
        e = jnp.exp(logits - jnp.max(logits, axis=-1, keepdims=True))
        aff = e / jnp.sum(e, axis=-1, keepdims=True)
        aff_ref[rs, :] = aff[:, :N_EXPERTS]


def _mix_call(x2d, ln_in_g, ln_in_b, attn, y2, z, norm_w, w_out, ln_g, ln_b, wr_hi, wr_lo):
    tokens = x2d.shape[0]
    tm = TOKEN_TILE
    row = lambda i: (i, 0)
    fixed = lambda i: (0, 0)
    vec = pl.BlockSpec((1, D_MODEL), fixed)
    return pl.pallas_call(
        _mix_kernel,
        grid=(tokens // tm,),
        in_specs=[
            pl.BlockSpec((tm, D_MODEL), row), vec, vec,
            pl.BlockSpec((tm, ATTN_W), row),
            pl.BlockSpec((None, tm, D_INNER), lambda i: (0, i, 0)),
            pl.BlockSpec((None, tm, D_INNER), lambda i: (1, i, 0)),
            pl.BlockSpec((tm, D_INNER), row),
            pl.BlockSpec((1, D_INNER), fixed),
            pl.BlockSpec((D_MODEL, D_MODEL), fixed),
            vec, vec,
            pl.BlockSpec((D_MODEL, LANES), fixed),
            pl.BlockSpec((D_MODEL, LANES), fixed),
        ],
        out_specs=[
            pl.BlockSpec((tm, D_MODEL), row),
            pl.BlockSpec((tm, N_EXPERTS), row),
        ],
        out_shape=[
            jax.ShapeDtypeStruct((tokens, D_MODEL), F32),
            jax.ShapeDtypeStruct((tokens, N_EXPERTS), F32),
        ],
        compiler_params=_cparams(("parallel",)),
        name="mix",
    )(x2d, ln_in_g, ln_in_b, attn, y2, y2, z, norm_w, w_out, ln_g, ln_b, wr_hi, wr_lo)


def _tile_cumsum(mask_bf, upper, strict_lower):
    wloc = _dot(mask_bf, upper)
    tot = jnp.broadcast_to(wloc[:, LANES - 1:LANES], wloc.shape).astype(BF16)
    offs = _dot(strict_lower, tot)
    return wloc, offs


def _select_kernel(aff_ref, idx_ref, lp_ref, win_ref, *, cap):
    nt = aff_ref.shape[0]
    aff = aff_ref[...]

    t_bits = jnp.zeros((1, 1), I32)
    for bit in range(30, -1, -1):
        cand = t_bits | (1 << bit)
        cnt = jnp.sum(jnp.where(aff >= pltpu.bitcast(cand, F32), 1.0, 0.0), keepdims=True)
        t_bits = jnp.where(cnt >= cap, cand, t_bits)
    t = pltpu.bitcast(t_bits, F32)
    gt = aff > t
    eq = aff == t
    need = cap - jnp.sum(jnp.where(gt, 1.0, 0.0), keepdims=True)

    ri = lax.broadcasted_iota(I32, (LANES, LANES), 0)
    ci = lax.broadcasted_iota(I32, (LANES, LANES), 1)
    upper = jnp.where(ri <= ci, 1.0, 0.0).astype(BF16)
    rt = lax.broadcasted_iota(I32, (nt, nt), 0)
    ct = lax.broadcasted_iota(I32, (nt, nt), 1)
    strict_lower = jnp.where(ct < rt, 1.0, 0.0).astype(BF16)

    eq_loc, eq_offs = _tile_cumsum(jnp.where(eq, 1.0, 0.0).astype(BF16), upper, strict_lower)
    sel = gt | (eq & (eq_loc + eq_offs <= need))
    sel_bf = jnp.where(sel, 1.0, 0.0).astype(BF16)
    wloc, offs = _tile_cumsum(sel_bf, upper, strict_lower)

    sel_tiles = COMB_TILE // LANES
    first_lower = jnp.where(ct < (rt // sel_tiles) * sel_tiles, 1.0, 0.0).astype(BF16)
    offs_first = _dot(first_lower, jnp.broadcast_to(wloc[:, LANES - 1:LANES], wloc.shape).astype(BF16))
    win = jnp.minimum(jnp.floor(offs_first * (1.0 / BF16_SUBLANES)) * BF16_SUBLANES, float(cap - COMB_SPAN))
    win_ref[...] = win
    lp_ref[...] = jnp.where(sel, wloc + offs - 1.0 - win, -1.0)

    tot_row = _dot_nt(jnp.ones((8, LANES), BF16), sel_bf)
    upper_t = jnp.where(rt <= ct, 1.0, 0.0).astype(BF16)
    incl_row = _dot(tot_row.astype(BF16), upper_t)[0:1, :]
    excl_row = incl_row - tot_row[0:1, :]
    tile_id = lax.broadcasted_iota(I32, (nt, LANES), 0)
    lane_id = lax.broadcasted_iota(I32, (nt, LANES), 1)
    offs_hi = jnp.floor(offs * (1.0 / LANES))
    side = jnp.where(lane_id == 0, (tile_id // 2).astype(F32),
           jnp.where(lane_id == 1, (tile_id % 2).astype(F32),
           jnp.where(lane_id == 2, offs_hi,
           jnp.where(lane_id == 3, offs - offs_hi * LANES, 0.0))))
    table = jnp.concatenate([wloc, side], axis=-1).astype(BF16)
    for s0 in range(0, cap, SEL_SLOT_CHUNK):
        n = min(SEL_SLOT_CHUNK, cap - s0)
        slot = (lax.broadcasted_iota(I32, (n, 1), 0) + s0).astype(F32)
        onehot = jnp.where((excl_row <= slot) & (slot < incl_row), 1.0, 0.0).astype(BF16)
        got = _dot(onehot, table)
        tile = 2.0 * got[:, LANES:LANES + 1] + got[:, LANES + 1:LANES + 2]
        base = LANES * got[:, LANES + 2:LANES + 3] + got[:, LANES + 3:LANES + 4]
        lane = jnp.sum(jnp.where(got[:, :LANES] <= slot - base, 1.0, 0.0), axis=-1, keepdims=True)
        tok = jnp.broadcast_to(tile * LANES + lane, (n, LANES))
        idx_ref[:, s0:s0 + n] = tok.T[0:1, :].astype(I32)


def _select_call(aff):
    tokens = aff.shape[0]
    cap = CAPACITY_FACTOR * tokens // N_EXPERTS
    nt = tokens // LANES
    aff_t = aff.T.reshape(N_EXPERTS, nt, LANES)
    blk = pl.BlockSpec((None, nt, LANES), lambda e: (e, 0, 0))
    idx, lp, win = pl.pallas_call(
        functools.partial(_select_kernel, cap=cap),
        grid=(N_EXPERTS,),
        in_specs=[blk],
        out_specs=[pl.BlockSpec((None, 1, cap), lambda e: (e, 0, 0)), blk, blk],
        out_shape=[
            jax.ShapeDtypeStruct((N_EXPERTS, 1, cap), I32),
            jax.ShapeDtypeStruct((N_EXPERTS, nt, LANES), F32),
            jax.ShapeDtypeStruct((N_EXPERTS, nt, LANES), F32),
        ],
        compiler_params=_cparams(("parallel",)),
        name="select",
    )(aff_t)
    lp_tok = lp.reshape(N_EXPERTS, tokens).T
    win_start = win[:, ::COMB_TILE // LANES, 0].astype(I32).reshape(-1)
    return idx.reshape(N_EXPERTS * cap), lp_tok, win_start, cap


def _ffn_kernel(idx_ref, x_hbm, wg_ref, wu_ref, wd_ref, ye_ref, land_a, land_b, xe_a, xe_b, sem_ref,
                *, cap):
    e = pl.program_id(0)
    c = pl.program_id(1)
    pairs = cap // (2 * FFN_ROWS)
    step = e * pairs + c
    n_steps = N_EXPERTS * pairs
    last = 2 * n_steps - 1
    land = (land_a, land_b)
    stage = (xe_a, xe_b)

    def row_copy(chunk, r, half):
        tok = idx_ref[chunk * FFN_ROWS + r]
        return pltpu.make_async_copy(x_hbm.at[pl.ds(tok, 1), :], land[half].at[pl.ds(r, 1), :],
                                     sem_ref.at[half])

    def start_chunk(chunk, half):
        for r in range(FFN_ROWS):
            row_copy(chunk, r, half).start()

    def wait_chunk(chunk, half):
        def body(r, carry):
            row_copy(chunk, r, half).wait()
            return carry
        lax.fori_loop(0, FFN_ROWS, body, 0, unroll=8)

    @pl.when(step == 0)
    def _():
        start_chunk(0, 0)
        start_chunk(1, 1)

    for half in range(2):
        wait_chunk(2 * step + half, half)
        stage[half][...] = land[half][...].astype(BF16)
        start_chunk(jnp.minimum(2 * step + 2 + half, last), half)
        xe = stage[half][...]
        gate = _dot(xe, wg_ref[...])
        up = _dot(xe, wu_ref[...])
        hid = (gate * _sigmoid(gate) * up).astype(BF16)
        ye_ref[half * FFN_ROWS:(half + 1) * FFN_ROWS, :] = _dot(hid, wd_ref[...]).astype(BF16)

    @pl.when(step == n_steps - 1)
    def _():
        wait_chunk(last, 0)
        wait_chunk(last, 1)


def _ffn_call(idx, x1, wg, wu, wd, cap):
    pairs = cap // (2 * FFN_ROWS)
    wspec = lambda d0, d1: pl.BlockSpec((None, d0, d1), lambda e, c, idx: (e, 0, 0))
    grid_spec = pltpu.PrefetchScalarGridSpec(
        num_scalar_prefetch=1,
        grid=(N_EXPERTS, pairs),
        in_specs=[
            pl.BlockSpec(memory_space=pl.ANY),
            wspec(D_MODEL, D_EXPERT), wspec(D_MODEL, D_EXPERT), wspec(D_EXPERT, D_MODEL),
        ],
        out_specs=pl.BlockSpec((None, 2 * FFN_ROWS, D_MODEL), lambda e, c, idx: (e, c, 0)),
        scratch_shapes=[
            pltpu.VMEM((FFN_ROWS, D_MODEL), F32),
            pltpu.VMEM((FFN_ROWS, D_MODEL), F32),
            pltpu.VMEM((FFN_ROWS, D_MODEL), BF16),
            pltpu.VMEM((FFN_ROWS, D_MODEL), BF16),
            pltpu.SemaphoreType.DMA((2,)),
        ],
    )
    return pl.pallas_call(
        functools.partial(_ffn_kernel, cap=cap),
        grid_spec=grid_spec,
        out_shape=jax.ShapeDtypeStruct((N_EXPERTS, cap, D_MODEL), BF16),
        compiler_params=_cparams(("arbitrary", "arbitrary")),
        name="ffn",
    )(idx, x1, wg, wu, wd)


def _combine_kernel(win_ref, x1_ref, p_ref, lp_ref, aff_ref, ye_hbm, wpg_ref, bpg_ref, wple_ref,
                    g_ref, b_ref, o_ref, buf_ref, xbuf_ref, acc_ref, sem_ref, xsem_ref, *, n_tiles):
    i = pl.program_id(0)
    slot = i % 2
    width = N_EXPERTS * COMB_WIN

    def win_copy(tile, e, rnd, dst, sem):
        start = pl.multiple_of(win_ref[e * n_tiles + tile] + rnd * COMB_WIN, BF16_SUBLANES)
        return pltpu.make_async_copy(ye_hbm.at[e, pl.ds(start, COMB_WIN), :],
                                     dst.at[pl.ds(e * COMB_WIN, COMB_WIN), :], sem)

    def start_tile(tile, buf_slot):
        for e in range(N_EXPERTS):
            win_copy(tile, e, 0, buf_ref.at[buf_slot], sem_ref.at[buf_slot]).start()

    @pl.when(i == 0)
    def _():
        start_tile(0, 0)

    @pl.when(i + 1 < n_tiles)
    def _():
        start_tile(i + 1, 1 - slot)

    x1 = x1_ref[...]
    ple = _sigmoid(_dot(x1.astype(BF16), wpg_ref[...]) + bpg_ref[...]) \
        * _dot(p_ref[...].astype(BF16), wple_ref[...])
    acc_ref[...] = DN_ALPHA * x1 + ple

    lp = lp_ref[...]
    rnd_of = jnp.floor(lp * (1.0 / COMB_WIN))
    row_of = lp - rnd_of * COMB_WIN
    er = lax.broadcasted_iota(I32, (N_EXPERTS, width), 0)
    ec = lax.broadcasted_iota(I32, (N_EXPERTS, width), 1) // COMB_WIN
    expand = jnp.where(er == ec, 1.0, 0.0).astype(BF16)
    lane_row = (lax.broadcasted_iota(I32, (COMB_TILE, width), 1) % COMB_WIN).astype(F32)
    hit = _dot(row_of.astype(BF16), expand) == lane_row
    rnd_e = _dot(rnd_of.astype(BF16), expand)
    g_hi, g_lo = _split_bf16(aff_ref[...])
    g_hi_e = jnp.where(hit, _dot(g_hi, expand), 0.0)
    g_lo_e = jnp.where(hit, _dot(g_lo, expand), 0.0)

    def add_round(rnd, rows):
        pick = rnd_e == rnd
        lhs = jnp.concatenate([jnp.where(pick, g_hi_e, 0.0).astype(BF16),
                               jnp.where(pick, g_lo_e, 0.0).astype(BF16)], axis=0)
        both = _dot(lhs, rows)
        acc_ref[...] += both[:COMB_TILE] + both[COMB_TILE:]

    for e in range(N_EXPERTS):
        win_copy(i, e, 0, buf_ref.at[slot], sem_ref.at[slot]).wait()
    add_round(0.0, buf_ref[slot])

    n_rounds = jnp.max(rnd_of).astype(I32) + 1
    for rnd in range(1, COMB_ROUNDS):
        @pl.when(rnd < n_rounds)
        def _():
            for e in range(N_EXPERTS):
                win_copy(i, e, rnd, xbuf_ref, xsem_ref.at[0]).start()
            for e in range(N_EXPERTS):
                win_copy(i, e, rnd, xbuf_ref, xsem_ref.at[0]).wait()
            add_round(float(rnd), xbuf_ref[...])

    o_ref[...] = _layer_norm_rows(acc_ref[...], g_ref[...], b_ref[...])


def _combine_call(win_start, x1, p2d, lp_tok, aff, ye, wpg, bpg, wple, ln_g, ln_b):
    tokens = x1.shape[0]
    n_tiles = tokens // COMB_TILE
    row = lambda i, w: (i, 0)
    fixed = lambda i, w: (0, 0)
    vec = pl.BlockSpec((1, D_MODEL), fixed)
    grid_spec = pltpu.PrefetchScalarGridSpec(
        num_scalar_prefetch=1,
        grid=(n_tiles,),
        in_specs=[
            pl.BlockSpec((COMB_TILE, D_MODEL), row),
            pl.BlockSpec((COMB_TILE, PLE_DIM), row),
            pl.BlockSpec((COMB_TILE, N_EXPERTS), row),
            pl.BlockSpec((COMB_TILE, N_EXPERTS), row),
            pl.BlockSpec(memory_space=pl.ANY),
            pl.BlockSpec((D_MODEL, D_MODEL), fixed),
            vec,
            pl.BlockSpec((PLE_DIM, D_MODEL), fixed),
            vec, vec,
        ],
        out_specs=pl.BlockSpec((COMB_TILE, D_MODEL), row),
        scratch_shapes=[
            pltpu.VMEM((2, N_EXPERTS * COMB_WIN, D_MODEL), BF16),
            pltpu.VMEM((N_EXPERTS * COMB_WIN, D_MODEL), BF16),
            pltpu.VMEM((COMB_TILE, D_MODEL), F32),
            pltpu.SemaphoreType.DMA((2,)),
            pltpu.SemaphoreType.DMA((1,)),
        ],
    )
    return pl.pallas_call(
        functools.partial(_combine_kernel, n_tiles=n_tiles),
        grid_spec=grid_spec,
        out_shape=jax.ShapeDtypeStruct((tokens, D_MODEL), F32),
        compiler_params=_cparams(("arbitrary",)),
        name="combine",
    )(win_start, x1, p2d, lp_tok, aff, ye, wpg, bpg, wple, ln_g, ln_b)


def _prep_weights(ln_in_g, ln_in_b, w_in, q_norm_w, k_norm_w, conv_w, conv_b, dt_bias, a_log,
                  d_skip, ssd_norm_w, w_out, ln1_g, ln1_b, w_router, w_e_gate, w_e_up, w_e_down,
                  w_ple_gate, b_ple_gate, w_ple, ln2_g, ln2_b):
    l = 0
    vec = lambda a: a.reshape(1, -1)
    w_in_pad = jnp.zeros((D_MODEL, IN_COLS_PAD), F32).at[:, :IN_COLS].set(w_in[l]).astype(BF16)
    wr = jnp.zeros((D_MODEL, LANES), F32).at[:, :N_EXPERTS].set(w_router[l])
    wr_hi, wr_lo = _split_bf16(wr)
    return dict(
        ln_in_g=vec(ln_in_g), ln_in_b=vec(ln_in_b), w_in=w_in_pad,
        qw=vec(jnp.tile(q_norm_w[l], N_Q_HEADS)), kw=vec(jnp.tile(k_norm_w[l], N_KV_HEADS)),
        conv_w=conv_w[l], conv_b=conv_b[l], dt_bias=dt_bias[l], a_log=a_log[l], d_skip=d_skip[l],
        norm_w=vec(ssd_norm_w[l]), w_out=w_out[l].astype(BF16), ln1_g=vec(ln1_g[l]), ln1_b=vec(ln1_b[l]),
        wr_hi=wr_hi, wr_lo=wr_lo,
        wg=w_e_gate[l].astype(BF16), wu=w_e_up[l].astype(BF16), wd=w_e_down[l].astype(BF16),
        wpg=w_ple_gate[l].astype(BF16), bpg=vec(b_ple_gate[l]), wple=w_ple[l].astype(BF16),
        ln2_g=vec(ln2_g[l]), ln2_b=vec(ln2_b[l]),
    )


def _trunk(x, p, w):
    batch, seq, _ = x.shape
    tokens = batch * seq
    x2d = x.reshape(tokens, D_MODEL)
    q, k, v, z, xbc, dt2 = _inproj_call(x2d, seq, w["ln_in_g"], w["ln_in_b"], w["w_in"], w["qw"], w["kw"])
    attn = _attn_call(q, k, v, batch, seq)
    y2 = _ssd_call(xbc, dt2, batch, seq, w["conv_w"], w["conv_b"], w["dt_bias"], w["a_log"], w["d_skip"])
    x1, aff = _mix_call(x2d, w["ln_in_g"], w["ln_in_b"], attn, y2, z, w["norm_w"], w["w_out"],
                        w["ln1_g"], w["ln1_b"], w["wr_hi"], w["wr_lo"])
    idx, lp_tok, win_start, cap = _select_call(aff)
    ye = _ffn_call(idx, x1, w["wg"], w["wu"], w["wd"], cap)
    out = _combine_call(win_start, x1, p[0].reshape(tokens, PLE_DIM), lp_tok, aff, ye,
                        w["wpg"], w["bpg"], w["wple"], w["ln2_g"], w["ln2_b"])
    return out.reshape(batch, seq, D_MODEL)


def kernel(x_prompt, x_sample, p_prompt, p_sample, ln_in_g, ln_in_b, w_in, q_norm_w, k_norm_w, conv_w, conv_b, dt_bias, a_log, d_skip, ssd_norm_w, w_out, ln1_g, ln1_b, w_router, w_e_gate, w_e_up, w_e_down, w_ple_gate, b_ple_gate, w_ple, ln2_g, ln2_b):
    w = _prep_weights(ln_in_g, ln_in_b, w_in, q_norm_w, k_norm_w, conv_w, conv_b, dt_bias, a_log,
                      d_skip, ssd_norm_w, w_out, ln1_g, ln1_b, w_router, w_e_gate, w_e_up, w_e_down,
                      w_ple_gate, b_ple_gate, w_ple, ln2_g, ln2_b)
    return (_trunk(x_prompt, p_prompt, w), _trunk(x_sample, p_sample, w))
```

```python
import functools
import math

import jax
import jax.numpy as jnp
from jax import lax
from jax.experimental import pallas as pl
from jax.experimental.pallas import tpu as pltpu

F32 = jnp.float32
BF16 = jnp.bfloat16
I32 = jnp.int32

D_MODEL = 1024
DEPTH = 1
GRID_W = 64
N_Q_HEADS = 8
N_KV_HEADS = 2
HEAD_DIM = 64
Q_PER_KV = N_Q_HEADS // N_KV_HEADS
ATTN_W = N_Q_HEADS * HEAD_DIM
KV_W = N_KV_HEADS * HEAD_DIM
ROPE_THETA = 10000.0
SSD_HEADS = 8
SSD_HEAD_DIM = 64
D_INNER = SSD_HEADS * SSD_HEAD_DIM
SSD_GROUPS = 2
HEADS_PER_GROUP = SSD_HEADS // SSD_GROUPS
D_STATE = 128
D_CONV = 5
CHUNK = 128
CONV_CH = D_INNER + 2 * SSD_GROUPS * D_STATE
K0 = ATTN_W
V0 = K0 + KV_W
Z0 = V0 + KV_W
XBC0 = Z0 + D_INNER
DT0 = XBC0 + CONV_CH
IN_COLS = DT0 + 2 * SSD_HEADS
N_EXPERTS = 16
CAPACITY_FACTOR = 2
D_EXPERT = 1024
PLE_DIM = 256
DN_ALPHA = (2.0 * DEPTH) ** 0.25
NORM_EPS = 1e-6
LN_EPS = 1e-5

LANES = 128
BF16_SUBLANES = 16
IN_COLS_PAD = DT0 + LANES
TOKEN_TILE = 512
ROW_GROUPS = 2
HALO = 8
SSD_BLOCK = 4
ATTN_Q_TILE = 256
FFN_ROWS = 256
COMB_TILE = 256
COMB_WIN = 64
COMB_ROUNDS = -(-(COMB_TILE + BF16_SUBLANES - 1) // COMB_WIN)
COMB_SPAN = COMB_ROUNDS * COMB_WIN
SEL_SLOT_CHUNK = 1024
VMEM_LIMIT = 56 * 1024 * 1024


def _cparams(sem):
    return pltpu.CompilerParams(dimension_semantics=sem, vmem_limit_bytes=VMEM_LIMIT)


def _layer_norm_rows(x, g, b):
    xc = x - jnp.mean(x, axis=-1, keepdims=True)
    return xc * lax.rsqrt(jnp.mean(xc * xc, axis=-1, keepdims=True) + LN_EPS) * g + b


def _sigmoid(x):
    return 1.0 / (1.0 + jnp.exp(-x))


def _dot(a, b):
    return jnp.dot(a, b, preferred_element_type=F32)


def _dot_nt(a, b):
    return lax.dot_general(a, b, (((1,), (1,)), ((), ())), preferred_element_type=F32)


def _split_bf16(x):
    hi = x.astype(BF16)
    lo = (x - hi.astype(F32)).astype(BF16)
    return hi, lo


def _split3_bf16(x):
    hi = x.astype(BF16)
    rest = x - hi.astype(F32)
    mid = rest.astype(BF16)
    lo = (rest - mid.astype(F32)).astype(BF16)
    return hi, mid, lo


def _rope_swap(x):
    outs = []
    for c in range(x.shape[-1] // LANES):
        xc = x[:, c * LANES:(c + 1) * LANES]
        lane = lax.broadcasted_iota(I32, xc.shape, 1)
        up = pltpu.roll(xc, LANES - 16, axis=1)
        dn = pltpu.roll(xc, 16, axis=1)
        outs.append(jnp.where((lane % 32) < 16, up, dn))
    return outs[0] if len(outs) == 1 else jnp.concatenate(outs, axis=-1)


def _head_rms(x, w):
    width = x.shape[-1]
    r = lax.broadcasted_iota(I32, (width, width), 0) // HEAD_DIM
    c = lax.broadcasted_iota(I32, (width, width), 1) // HEAD_DIM
    blockdiag = jnp.where(r == c, 1.0, 0.0).astype(BF16)
    hi, lo = _split_bf16(x * x)
    ssq = _dot(hi, blockdiag) + _dot(lo, blockdiag)
    return x * lax.rsqrt(ssq * (1.0 / HEAD_DIM) + NORM_EPS) * w


def _inproj_kernel(x_ref, xprev_ref, xnext_ref, g_ref, b_ref, w_ref, qw_ref, kw_ref, cos_ref, sin_ref,
                   cw_ref, cb_ref, q_ref, k_ref, v_ref, z_ref, xc_ref, dt_ref, pad_ref, *, tiles_per_seq):
    tm = x_ref.shape[0]
    sub = tm // ROW_GROUPS

    pos = pl.program_id(0) % tiles_per_seq
    has_prev = (pos > 0).astype(F32)
    has_next = (pos < tiles_per_seq - 1).astype(F32)
    edge = jnp.concatenate([xprev_ref[...], xnext_ref[...]], axis=0)
    edge = _dot(_layer_norm_rows(edge, g_ref[...], b_ref[...]).astype(BF16), w_ref[:, XBC0:DT0])
    pad_ref[0:HALO, :] = edge[:HALO] * has_prev
    pad_ref[HALO + tm:, :] = edge[HALO:] * has_next

    for grp in range(ROW_GROUPS):
        rs = slice(grp * sub, (grp + 1) * sub)
        xln = _layer_norm_rows(x_ref[rs, :], g_ref[...], b_ref[...])
        proj = _dot(xln.astype(BF16), w_ref[...])
        cos = cos_ref[rs, :]
        sin = sin_ref[rs, :]
        q = _head_rms(proj[:, :K0], qw_ref[...])
        cos_q = jnp.concatenate([cos] * (ATTN_W // LANES), axis=-1)
        sin_q = jnp.concatenate([sin] * (ATTN_W // LANES), axis=-1)
        q = (q * cos_q + _rope_swap(q) * sin_q) * (HEAD_DIM ** -0.5)
        q_ref[rs, :] = q.astype(BF16)
        k = _head_rms(proj[:, K0:V0], kw_ref[...])
        k = (k * cos + _rope_swap(k) * sin).astype(BF16)
        v = proj[:, V0:Z0]
        ones_col = jnp.where(lax.broadcasted_iota(I32, (sub, LANES - HEAD_DIM), 1) == 0, 1.0, 0.0)
        for g in range(N_KV_HEADS):
            k_ref[g, rs, :] = k[:, g * HEAD_DIM:(g + 1) * HEAD_DIM]
            v_ref[g, rs, :] = jnp.concatenate([v[:, g * HEAD_DIM:(g + 1) * HEAD_DIM], ones_col],
                                              axis=-1).astype(BF16)
        z_ref[rs, :] = proj[:, Z0:XBC0]
        pad_ref[HALO + grp * sub:HALO + (grp + 1) * sub, :] = proj[:, XBC0:DT0]
        dt = proj[:, DT0:DT0 + LANES]
        for d in range(2):
            dt_ref[d, rs, :] = dt[:, d * SSD_HEADS:(d + 1) * SSD_HEADS]

    cw = cw_ref[...]
    for grp in range(ROW_GROUPS):
        window = pad_ref[grp * sub:grp * sub + sub + 2 * HALO, :]
        acc = jnp.broadcast_to(cb_ref[...], (sub, CONV_CH))
        for tap in range(D_CONV):
            shift = (D_CONV // 2 - tap) % (sub + 2 * HALO)
            shifted = window if shift == 0 else pltpu.roll(window, shift, axis=0)
            acc = acc + shifted[HALO:HALO + sub, :] * cw[tap:tap + 1, :]
        xc_ref[grp * sub:(grp + 1) * sub, :] = acc * _sigmoid(acc)


def _rope_tables(seq):
    rows = seq // GRID_W
    row_id = jnp.broadcast_to(jnp.arange(rows)[:, None], (rows, GRID_W)).reshape(seq)
    col_id = jnp.broadcast_to(jnp.arange(GRID_W)[None, :], (rows, GRID_W)).reshape(seq)
    pos = jnp.stack([row_id, col_id], axis=-1).astype(F32)
    n_freq = HEAD_DIM // 4
    inv_freq = ROPE_THETA ** (-jnp.arange(n_freq, dtype=F32) / n_freq)
    ang = pos[:, :, None] * inv_freq
    cos = jnp.broadcast_to(jnp.cos(ang)[:, :, None, :], (seq, 2, 2, n_freq)).reshape(seq, HEAD_DIM)
    sin = jnp.broadcast_to(jnp.sin(ang)[:, :, None, :], (seq, 2, 2, n_freq))
    sin = (sin * jnp.array([-1.0, 1.0], F32)[None, None, :, None]).reshape(seq, HEAD_DIM)
    reps = LANES // HEAD_DIM
    return jnp.tile(cos, (1, reps)), jnp.tile(sin, (1, reps))


def _inproj_call(x2d, seq, ln_g, ln_b, w_in_pad, qw, kw, conv_w, conv_b):
    tokens = x2d.shape[0]
    tm = TOKEN_TILE
    nt = tokens // tm
    tiles_per_seq = seq // tm
    halo_per_tile = tm // HALO
    n_halo = tokens // HALO
    cos, sin = _rope_tables(seq)
    cw = jnp.zeros((8, CONV_CH), F32).at[:D_CONV].set(conv_w)
    cb = conv_b.reshape(1, CONV_CH)
    row = lambda i: (i, 0)
    fixed = lambda i: (0, 0)
    return pl.pallas_call(
        functools.partial(_inproj_kernel, tiles_per_seq=tiles_per_seq),
        grid=(nt,),
        in_specs=[
            pl.BlockSpec((tm, D_MODEL), row),
            pl.BlockSpec((HALO, D_MODEL), lambda i: (jnp.maximum(i * halo_per_tile - 1, 0), 0)),
            pl.BlockSpec((HALO, D_MODEL), lambda i: (jnp.minimum((i + 1) * halo_per_tile, n_halo - 1), 0)),
            pl.BlockSpec((1, D_MODEL), fixed),
            pl.BlockSpec((1, D_MODEL), fixed),
            pl.BlockSpec((D_MODEL, IN_COLS_PAD), fixed),
            pl.BlockSpec((1, ATTN_W), fixed),
            pl.BlockSpec((1, KV_W), fixed),
            pl.BlockSpec((tm, LANES), lambda i: (i % tiles_per_seq, 0)),
            pl.BlockSpec((tm, LANES), lambda i: (i % tiles_per_seq, 0)),
            pl.BlockSpec((8, CONV_CH), fixed),
            pl.BlockSpec((1, CONV_CH), fixed),
        ],
        out_specs=[
            pl.BlockSpec((tm, ATTN_W), row),
            pl.BlockSpec((N_KV_HEADS, tm, HEAD_DIM), lambda i: (0, i, 0)),
            pl.BlockSpec((N_KV_HEADS, tm, LANES), lambda i: (0, i, 0)),
            pl.BlockSpec((tm, D_INNER), row),
            pl.BlockSpec((tm, CONV_CH), row),
            pl.BlockSpec((2, tm, SSD_HEADS), lambda i: (0, i, 0)),
        ],
        out_shape=[
            jax.ShapeDtypeStruct((tokens, ATTN_W), BF16),
            jax.ShapeDtypeStruct((N_KV_HEADS, tokens, HEAD_DIM), BF16),
            jax.ShapeDtypeStruct((N_KV_HEADS, tokens, LANES), BF16),
            jax.ShapeDtypeStruct((tokens, D_INNER), F32),
            jax.ShapeDtypeStruct((tokens, CONV_CH), F32),
            jax.ShapeDtypeStruct((2, tokens, SSD_HEADS), F32),
        ],
        scratch_shapes=[pltpu.VMEM((tm + 2 * HALO, CONV_CH), F32)],
        compiler_params=_cparams(("parallel",)),
        name="inproj",
    )(x2d, x2d, x2d, ln_g, ln_b, w_in_pad, qw, kw, cos, sin, cw, cb)


def _attn_kernel(q_ref, k_ref, v_ref, o_ref):
    k = k_ref[...]
    v = v_ref[...]
    q = q_ref[...]
    outs = []
    for h in range(Q_PER_KV):
        s = _dot_nt(q[:, h * HEAD_DIM:(h + 1) * HEAD_DIM], k)
        m = jnp.max(s, axis=-1, keepdims=True)
        p = jnp.exp((s - m).astype(BF16))
        o = _dot(p, v)
        outs.append(o[:, :HEAD_DIM] / o[:, HEAD_DIM:HEAD_DIM + 1])
    o_ref[...] = jnp.concatenate(outs, axis=-1).astype(BF16)


def _attn_call(q, k, v, batch, seq):
    tokens = batch * seq
    tq = ATTN_Q_TILE
    nq = seq // tq
    width = Q_PER_KV * HEAD_DIM
    return pl.pallas_call(
        _attn_kernel,
        grid=(batch, N_KV_HEADS, nq),
        in_specs=[
            pl.BlockSpec((tq, width), lambda b, g, i: (b * nq + i, g)),
            pl.BlockSpec((None, seq, HEAD_DIM), lambda b, g, i: (g, b, 0)),
            pl.BlockSpec((None, seq, LANES), lambda b, g, i: (g, b, 0)),
        ],
        out_specs=pl.BlockSpec((tq, width), lambda b, g, i: (b * nq + i, g)),
        out_shape=jax.ShapeDtypeStruct((tokens, ATTN_W), BF16),
        compiler_params=_cparams(("parallel", "parallel", "parallel")),
        name="attn",
    )(q, k, v)


def _ssd_kernel(xc_ref, dt_ref, dtt_ref, dtb_ref, dtbt_ref, alog_ref, alogt_ref, dskip_ref,
                y_ref, state_ref):
    fwd = pl.program_id(1) == 0

    @pl.when(pl.program_id(2) == 0)
    def _():
        state_ref[...] = jnp.zeros_like(state_ref)

    for j in range(SSD_BLOCK):
        sub = jnp.where(fwd, j, SSD_BLOCK - 1 - j)
        _ssd_chunk(pl.multiple_of(sub * CHUNK, CHUNK), fwd, xc_ref, dt_ref, dtt_ref, dtb_ref, dtbt_ref,
                   alog_ref, alogt_ref, dskip_ref, y_ref, state_ref)


def _ssd_chunk(row0, fwd, xc_ref, dt_ref, dtt_ref, dtb_ref, dtbt_ref, alog_ref, alogt_ref, dskip_ref,
               y_ref, state_ref):
    rows = pl.ds(row0, CHUNK)
    xc = xc_ref[rows, :]
    xs = xc[:, :D_INNER]
    b_all = xc[:, D_INNER:D_INNER + SSD_GROUPS * D_STATE]
    c_all = xc[:, D_INNER + SSD_GROUPS * D_STATE:]

    dt = jax.nn.softplus(dt_ref[rows, :] + dtb_ref[...])
    a = dt * -jnp.exp(alog_ref[...])
    a_row = jax.nn.softplus(dtt_ref[:, rows] + dtbt_ref[...]) * -jnp.exp(alogt_ref[...])

    ri = lax.broadcasted_iota(I32, (CHUNK, CHUNK), 0)
    ci = lax.broadcasted_iota(I32, (CHUNK, CHUNK), 1)
    ahead = jnp.where(fwd, ri - ci, ci - ri)
    causal = ahead >= 0
    tri = jnp.where(causal, 1.0, 0.0).astype(BF16)
    tri_t = jnp.where(ahead <= 0, 1.0, 0.0).astype(BF16)
    cum = sum(_dot(tri, piece) for piece in _split3_bf16(a))
    cum_row = sum(_dot(piece, tri_t) for piece in _split3_bf16(a_row))
    total = jnp.sum(a, axis=0, keepdims=True)

    er = lax.broadcasted_iota(I32, (SSD_HEADS, D_INNER), 0)
    ec = lax.broadcasted_iota(I32, (SSD_HEADS, D_INNER), 1) // SSD_HEAD_DIM
    expand = jnp.where(er == ec, 1.0, 0.0).astype(BF16)
    stacked = jnp.concatenate([dt, jnp.exp(total - cum), jnp.exp(cum)], axis=0)
    stacked_e = sum(_dot(piece, expand) for piece in _split_bf16(stacked))
    dt_e = stacked_e[:CHUNK]
    dec_state_e = stacked_e[CHUNK:2 * CHUNK]
    dec_out_e = stacked_e[2 * CHUNK:]
    dec_chunk_e = sum(_dot(piece, expand) for piece in _split_bf16(jnp.exp(total)))

    xdt = xs * dt_e
    xdt_bf = xdt.astype(BF16)
    xdec_bf = (xdt * dec_state_e).astype(BF16)

    ys = []
    for g in range(SSD_GROUPS):
        bg = b_all[:, g * D_STATE:(g + 1) * D_STATE]
        cg = c_all[:, g * D_STATE:(g + 1) * D_STATE].astype(BF16)
        cb = _dot_nt(cg, bg.astype(BF16))
        lo = g * HEADS_PER_GROUP * SSD_HEAD_DIM
        hi = lo + HEADS_PER_GROUP * SSD_HEAD_DIM
        prev = state_ref[g]
        y_off = _dot(cg, prev.astype(BF16)) * dec_out_e[:, lo:hi]
        y_heads = []
        for r in range(HEADS_PER_GROUP):
            h = g * HEADS_PER_GROUP + r
            seg = cum[:, h:h + 1] - cum_row[h:h + 1, :]
            m = jnp.where(causal, cb * jnp.exp(jnp.where(causal, seg, 0.0)), 0.0)
            y_heads.append(_dot(m.astype(BF16), xdt_bf[:, h * SSD_HEAD_DIM:(h + 1) * SSD_HEAD_DIM]))
        ys.append(jnp.concatenate(y_heads, axis=-1) + y_off)
        st = _dot(bg.T.astype(BF16), xdec_bf[:, lo:hi])
        state_ref[g] = prev * dec_chunk_e[:, lo:hi] + st
    y = jnp.concatenate(ys, axis=-1)
    y_ref[rows, :] = y + jnp.where(fwd, 1.0, 0.0) * (dskip_ref[...] * xs)


def _ssd_call(xc, dt2, batch, seq, dt_bias, a_log, d_skip):
    tokens = batch * seq
    rows = SSD_BLOCK * CHUNK
    nc = seq // rows
    dtt = dt2.reshape(2, batch, seq, SSD_HEADS).transpose(1, 0, 3, 2).reshape(batch * 2 * SSD_HEADS, seq)
    dtb = dt_bias.reshape(2, 1, SSD_HEADS)
    dtbt = dt_bias.reshape(2, SSD_HEADS, 1)
    na = a_log.reshape(2, 1, SSD_HEADS)
    nat = a_log.reshape(2, SSD_HEADS, 1)
    dskip_e = jnp.repeat(d_skip, SSD_HEAD_DIM).reshape(1, D_INNER)

    def ceff(d, c):
        return jnp.where(d == 0, c, nc - 1 - c)

    def main_idx(b, d, c):
        return (b * nc + ceff(d, c), 0)

    sel_d = lambda b, d, c: (d, 0, 0)
    fixed = lambda b, d, c: (0, 0)
    return pl.pallas_call(
        _ssd_kernel,
        grid=(batch, 2, nc),
        in_specs=[
            pl.BlockSpec((rows, CONV_CH), main_idx),
            pl.BlockSpec((None, rows, SSD_HEADS), lambda b, d, c: (d, b * nc + ceff(d, c), 0)),
            pl.BlockSpec((SSD_HEADS, rows), lambda b, d, c: (b * 2 + d, ceff(d, c))),
            pl.BlockSpec((None, 1, SSD_HEADS), sel_d),
            pl.BlockSpec((None, SSD_HEADS, 1), sel_d),
            pl.BlockSpec((None, 1, SSD_HEADS), sel_d),
            pl.BlockSpec((None, SSD_HEADS, 1), sel_d),
            pl.BlockSpec((1, D_INNER), fixed),
        ],
        out_specs=pl.BlockSpec((None, rows, D_INNER), lambda b, d, c: (d, b * nc + ceff(d, c), 0)),
        out_shape=jax.ShapeDtypeStruct((2, tokens, D_INNER), F32),
        scratch_shapes=[
            pltpu.VMEM((SSD_GROUPS, D_STATE, HEADS_PER_GROUP * SSD_HEAD_DIM), F32),
        ],
        compiler_params=_cparams(("parallel", "arbitrary", "arbitrary")),
        name="ssd",
    )(xc, dt2, dtt, dtb, dtbt, na, nat, dskip_e)


def _mix_kernel(x_ref, ing_ref, inb_ref, attn_ref, yf_ref, yb_ref, z_ref, nw_ref, wo_ref,
                g_ref, b_ref, wrh_ref, wrl_ref, x1_ref, aff_ref):
    sub = x_ref.shape[0] // ROW_GROUPS
    for grp in range(ROW_GROUPS):
        rs = slice(grp * sub, (grp + 1) * sub)
        xln = _layer_norm_rows(x_ref[rs, :], ing_ref[...], inb_ref[...])
        z = z_ref[rs, :]
        y = (yf_ref[rs, :] + yb_ref[rs, :]) * (z * _sigmoid(z))
        gw = D_INNER // SSD_GROUPS
        parts = []
        for g in range(SSD_GROUPS):
            yg = y[:, g * gw:(g + 1) * gw]
            parts.append(yg * lax.rsqrt(jnp.mean(yg * yg, axis=-1, keepdims=True) + NORM_EPS))
        ssd = (jnp.concatenate(parts, axis=-1) * nw_ref[...]).astype(BF16)
        mix = _dot(jnp.concatenate([attn_ref[rs, :], ssd], axis=-1), wo_ref[...])
        x1 = _layer_norm_rows(DN_ALPHA * xln + mix, g_ref[...], b_ref[...])
        x1_ref[rs, :] = x1
        hi, lo = _split_bf16(x1)
        wrh = wrh_ref[...]
        logits = _dot(hi, wrh) + _dot(lo, wrh) + _dot(hi, wrl_ref[...])
        lane = lax.broadcasted_iota(I32, logits.shape, 1)
        logits = jnp.where(lane < N_EXPERTS, logits, -jnp.inf)
        e = jnp.exp(logits - jnp.max(logits, axis=-1, keepdims=True))
        aff = e / jnp.sum(e, axis=-1, keepdims=True)
        aff_ref[rs, :] = aff[:, :N_EXPERTS]


def _mix_call(x2d, ln_in_g, ln_in_b, attn, y2, z, norm_w, w_out, ln_g, ln_b, wr_hi, wr_lo):
    tokens = x2d.shape[0]
    tm = TOKEN_TILE
    row = lambda i: (i, 0)
    fixed = lambda i: (0, 0)
    vec = pl.BlockSpec((1, D_MODEL), fixed)
    return pl.pallas_call(
        _mix_kernel,
        grid=(tokens // tm,),
        in_specs=[
            pl.BlockSpec((tm, D_MODEL), row), vec, vec,
            pl.BlockSpec((tm, ATTN_W), row),
            pl.BlockSpec((None, tm, D_INNER), lambda i: (0, i, 0)),
            pl.BlockSpec((None, tm, D_INNER), lambda i: (1, i, 0)),
            pl.BlockSpec((tm, D_INNER), row),
            pl.BlockSpec((1, D_INNER), fixed),
            pl.BlockSpec((D_MODEL, D_MODEL), fixed),
            vec, vec,
            pl.BlockSpec((D_MODEL, LANES), fixed),
            pl.BlockSpec((D_MODEL, LANES), fixed),
        ],
        out_specs=[
            pl.BlockSpec((tm, D_MODEL), row),
            pl.BlockSpec((tm, N_EXPERTS), row),
        ],
        out_shape=[
            jax.ShapeDtypeStruct((tokens, D_MODEL), F32),
            jax.ShapeDtypeStruct((tokens, N_EXPERTS), F32),
        ],
        compiler_params=_cparams(("parallel",)),
        name="mix",
    )(x2d, ln_in_g, ln_in_b, attn, y2, y2, z, norm_w, w_out, ln_g, ln_b, wr_hi, wr_lo)


def _tile_cumsum(mask_bf, upper, strict_lower):
    wloc = _dot(mask_bf, upper)
    tot = jnp.broadcast_to(wloc[:, LANES - 1:LANES], wloc.shape).astype(BF16)
    offs = _dot(strict_lower, tot)
    return wloc, offs


def _select_kernel(aff_ref, idx_ref, lp_ref, win_ref, *, cap):
    nt = aff_ref.shape[0]
    aff = aff_ref[...]

    def enough(cand_bits):
        cnt = jnp.sum(jnp.where(aff >= pltpu.bitcast(cand_bits, F32), 1.0, 0.0), keepdims=True)
        return cnt >= cap

    t_bits = jnp.where(enough(jnp.full((1, 1), 1 << 30, I32)), 1 << 30, 0).astype(I32)
    for bit in range(29, 0, -2):
        hi, lo = 1 << bit, 1 << (bit - 1)
        c1, c2, c3 = t_bits | lo, t_bits | hi, t_bits | hi | lo
        t_bits = jnp.where(enough(c3), c3, jnp.where(enough(c2), c2, jnp.where(enough(c1), c1, t_bits)))
    t = pltpu.bitcast(t_bits, F32)
    gt = aff > t
    eq = aff == t
    need = cap - jnp.sum(jnp.where(gt, 1.0, 0.0), keepdims=True)

    ri = lax.broadcasted_iota(I32, (LANES, LANES), 0)
    ci = lax.broadcasted_iota(I32, (LANES, LANES), 1)
    upper = jnp.where(ri <= ci, 1.0, 0.0).astype(BF16)
    rt = lax.broadcasted_iota(I32, (nt, nt), 0)
    ct = lax.broadcasted_iota(I32, (nt, nt), 1)
    strict_lower = jnp.where(ct < rt, 1.0, 0.0).astype(BF16)

    eq_loc, eq_offs = _tile_cumsum(jnp.where(eq, 1.0, 0.0).astype(BF16), upper, strict_lower)
    sel = gt | (eq & (eq_loc + eq_offs <= need))
    sel_bf = jnp.where(sel, 1.0, 0.0).astype(BF16)
    wloc, offs = _tile_cumsum(sel_bf, upper, strict_lower)

    sel_tiles = COMB_TILE // LANES
    first_lower = jnp.where(ct < (rt // sel_tiles) * sel_tiles, 1.0, 0.0).astype(BF16)
    offs_first = _dot(first_lower, jnp.broadcast_to(wloc[:, LANES - 1:LANES], wloc.shape).astype(BF16))
    win = jnp.minimum(jnp.floor(offs_first * (1.0 / BF16_SUBLANES)) * BF16_SUBLANES, float(cap - COMB_SPAN))
    win_ref[...] = win
    lp_ref[...] = jnp.where(sel, wloc + offs - 1.0 - win, -1.0)

    tot = jnp.broadcast_to(wloc[:, LANES - 1:LANES], wloc.shape)
    tot_row = _dot_nt(jnp.ones((8, LANES), BF16), sel_bf)
    upper_t = jnp.where(rt <= ct, 1.0, 0.0).astype(BF16)
    incl_row = _dot(tot_row.astype(BF16), upper_t)[0:1, :]
    next_tile = jnp.where(ct == rt + 1, 1.0, 0.0).astype(BF16)
    wloc_step = _dot(next_tile, wloc.astype(BF16)) - wloc
    lane_id = lax.broadcasted_iota(I32, (nt, LANES), 1)
    side = jnp.where(lane_id == 0, 1.0, jnp.where(lane_id == 1, tot, 0.0))
    steps = jnp.concatenate([wloc_step, side], axis=-1).astype(BF16)
    first = jnp.concatenate([wloc[0:1, :], jnp.zeros((1, LANES), F32)], axis=-1)
    for s0 in range(0, cap, SEL_SLOT_CHUNK):
        n = min(SEL_SLOT_CHUNK, cap - s0)
        slot = (lax.broadcasted_iota(I32, (n, 1), 0) + s0).astype(F32)
        before = jnp.where(incl_row <= slot, 1.0, 0.0).astype(BF16)
        got = _dot(before, steps) + first
        tile = got[:, LANES:LANES + 1]
        base = got[:, LANES + 1:LANES + 2]
        lane = jnp.sum(jnp.where(got[:, :LANES] <= slot - base, 1.0, 0.0), axis=-1, keepdims=True)
        tok = jnp.broadcast_to(tile * LANES + lane, (n, LANES))
        idx_ref[:, s0:s0 + n] = tok.T[0:1, :].astype(I32)


def _select_call(aff):
    tokens = aff.shape[0]
    cap = CAPACITY_FACTOR * tokens // N_EXPERTS
    nt = tokens // LANES
    aff_t = aff.T.reshape(N_EXPERTS, nt, LANES)
    blk = pl.BlockSpec((None, nt, LANES), lambda e: (e, 0, 0))
    idx, lp, win = pl.pallas_call(
        functools.partial(_select_kernel, cap=cap),
        grid=(N_EXPERTS,),
        in_specs=[blk],
        out_specs=[pl.BlockSpec((None, 1, cap), lambda e: (e, 0, 0)), blk, blk],
        out_shape=[
            jax.ShapeDtypeStruct((N_EXPERTS, 1, cap), I32),
            jax.ShapeDtypeStruct((N_EXPERTS, nt, LANES), F32),
            jax.ShapeDtypeStruct((N_EXPERTS, nt, LANES), F32),
        ],
        compiler_params=_cparams(("parallel",)),
        name="select",
    )(aff_t)
    lp_tok = lp.reshape(N_EXPERTS, tokens).T
    win_start = win[:, ::COMB_TILE // LANES, 0].astype(I32).reshape(-1)
    return idx.reshape(N_EXPERTS * cap), lp_tok, win_start, cap


def _ffn_kernel(idx_ref, x_hbm, wg_ref, wu_ref, wd_ref, ye_ref, land_a, land_b, xe_a, xe_b, sem_ref,
                *, cap):
    e = pl.program_id(0)
    c = pl.program_id(1)
    pairs = cap // (2 * FFN_ROWS)
    step = e * pairs + c
    n_steps = N_EXPERTS * pairs
    last = 2 * n_steps - 1
    land = (land_a, land_b)
    stage = (xe_a, xe_b)

    def row_copy(chunk, r, half):
        tok = idx_ref[chunk * FFN_ROWS + r]
        return pltpu.make_async_copy(x_hbm.at[pl.ds(tok, 1), :], land[half].at[pl.ds(r, 1), :],
                                     sem_ref.at[half])

    def start_chunk(chunk, half):
        for r in range(FFN_ROWS):
            row_copy(chunk, r, half).start()

    def wait_chunk(chunk, half):
        def body(r, carry):
            row_copy(chunk, r, half).wait()
            return carry
        lax.fori_loop(0, FFN_ROWS, body, 0, unroll=8)

    @pl.when(step == 0)
    def _():
        start_chunk(0, 0)
        start_chunk(1, 1)

    for half in range(2):
        wait_chunk(2 * step + half, half)
        stage[half][...] = land[half][...].astype(BF16)
        start_chunk(jnp.minimum(2 * step + 2 + half, last), half)
        xe = stage[half][...]
        gate = _dot(xe, wg_ref[...])
        up = _dot(xe, wu_ref[...])
        hid = (gate * _sigmoid(gate) * up).astype(BF16)
        ye_ref[half * FFN_ROWS:(half + 1) * FFN_ROWS, :] = _dot(hid, wd_ref[...]).astype(BF16)

    @pl.when(step == n_steps - 1)
    def _():
        wait_chunk(last, 0)
        wait_chunk(last, 1)


def _ffn_call(idx, x1, wg, wu, wd, cap):
    pairs = cap // (2 * FFN_ROWS)
    wspec = lambda d0, d1: pl.BlockSpec((None, d0, d1), lambda e, c, idx: (e, 0, 0))
    grid_spec = pltpu.PrefetchScalarGridSpec(
        num_scalar_prefetch=1,
        grid=(N_EXPERTS, pairs),
        in_specs=[
            pl.BlockSpec(memory_space=pl.ANY),
            wspec(D_MODEL, D_EXPERT), wspec(D_MODEL, D_EXPERT), wspec(D_EXPERT, D_MODEL),
        ],
        out_specs=pl.BlockSpec((None, 2 * FFN_ROWS, D_MODEL), lambda e, c, idx: (e, c, 0)),
        scratch_shapes=[
            pltpu.VMEM((FFN_ROWS, D_MODEL), F32),
            pltpu.VMEM((FFN_ROWS, D_MODEL), F32),
            pltpu.VMEM((FFN_ROWS, D_MODEL), BF16),
            pltpu.VMEM((FFN_ROWS, D_MODEL), BF16),
            pltpu.SemaphoreType.DMA((2,)),
        ],
    )
    return pl.pallas_call(
        functools.partial(_ffn_kernel, cap=cap),
        grid_spec=grid_spec,
        out_shape=jax.ShapeDtypeStruct((N_EXPERTS, cap, D_MODEL), BF16),
        compiler_params=_cparams(("arbitrary", "arbitrary")),
        name="ffn",
    )(idx, x1, wg, wu, wd)


def _combine_kernel(win_ref, x1_ref, p_ref, lp_ref, aff_ref, ye_hbm, wpg_ref, bpg_ref, wple_ref,
                    g_ref, b_ref, o_ref, buf_ref, xbuf_ref, acc_ref, sem_ref, xsem_ref, *, n_tiles):
    i = pl.program_id(0)
    slot = i % 2
    width = N_EXPERTS * COMB_WIN

    def win_copy(tile, e, rnd, dst, sem):
        start = pl.multiple_of(win_ref[e * n_tiles + tile] + rnd * COMB_WIN, BF16_SUBLANES)
        return pltpu.make_async_copy(ye_hbm.at[e, pl.ds(start, COMB_WIN), :],
                                     dst.at[pl.ds(e * COMB_WIN, COMB_WIN), :], sem)

    def start_tile(tile, buf_slot):
        for e in range(N_EXPERTS):
            win_copy(tile, e, 0, buf_ref.at[buf_slot], sem_ref.at[buf_slot]).start()

    @pl.when(i == 0)
    def _():
        start_tile(0, 0)

    @pl.when(i + 1 < n_tiles)
    def _():
        start_tile(i + 1, 1 - slot)

    x1 = x1_ref[...]
    ple = _sigmoid(_dot(x1.astype(BF16), wpg_ref[...]) + bpg_ref[...]) \
        * _dot(p_ref[...].astype(BF16), wple_ref[...])
    acc_ref[...] = DN_ALPHA * x1 + ple

    lp = lp_ref[...]
    rnd_of = jnp.floor(lp * (1.0 / COMB_WIN))
    row_of = lp - rnd_of * COMB_WIN
    er = lax.broadcasted_iota(I32, (N_EXPERTS, width), 0)
    ec = lax.broadcasted_iota(I32, (N_EXPERTS, width), 1) // COMB_WIN
    expand = jnp.where(er == ec, 1.0, 0.0).astype(BF16)
    lane_row = (lax.broadcasted_iota(I32, (COMB_TILE, width), 1) % COMB_WIN).astype(F32)
    hit = _dot(row_of.astype(BF16), expand) == lane_row
    rnd_e = _dot(rnd_of.astype(BF16), expand)
    g_hi, g_lo = _split_bf16(aff_ref[...])
    g_hi_e = jnp.where(hit, _dot(g_hi, expand), 0.0)
    g_lo_e = jnp.where(hit, _dot(g_lo, expand), 0.0)

    def add_round(rnd, rows):
        pick = rnd_e == rnd
        lhs = jnp.concatenate([jnp.where(pick, g_hi_e, 0.0).astype(BF16),
                               jnp.where(pick, g_lo_e, 0.0).astype(BF16)], axis=0)
        both = _dot(lhs, rows)
        acc_ref[...] += both[:COMB_TILE] + both[COMB_TILE:]

    for e in range(N_EXPERTS):
        win_copy(i, e, 0, buf_ref.at[slot], sem_ref.at[slot]).wait()
    add_round(0.0, buf_ref[slot])

    n_rounds = jnp.max(rnd_of).astype(I32) + 1
    for rnd in range(1, COMB_ROUNDS):
        @pl.when(rnd < n_rounds)
        def _():
            for e in range(N_EXPERTS):
                win_copy(i, e, rnd, xbuf_ref, xsem_ref.at[0]).start()
            for e in range(N_EXPERTS):
                win_copy(i, e, rnd, xbuf_ref, xsem_ref.at[0]).wait()
            add_round(float(rnd), xbuf_ref[...])

    o_ref[...] = _layer_norm_rows(acc_ref[...], g_ref[...], b_ref[...])


def _combine_call(win_start, x1, p2d, lp_tok, aff, ye, wpg, bpg, wple, ln_g, ln_b):
    tokens = x1.shape[0]
    n_tiles = tokens // COMB_TILE
    row = lambda i, w: (i, 0)
    fixed = lambda i, w: (0, 0)
    vec = pl.BlockSpec((1, D_MODEL), fixed)
    grid_spec = pltpu.PrefetchScalarGridSpec(
        num_scalar_prefetch=1,
        grid=(n_tiles,),
        in_specs=[
            pl.BlockSpec((COMB_TILE, D_MODEL), row),
            pl.BlockSpec((COMB_TILE, PLE_DIM), row),
            pl.BlockSpec((COMB_TILE, N_EXPERTS), row),
            pl.BlockSpec((COMB_TILE, N_EXPERTS), row),
            pl.BlockSpec(memory_space=pl.ANY),
            pl.BlockSpec((D_MODEL, D_MODEL), fixed),
            vec,
            pl.BlockSpec((PLE_DIM, D_MODEL), fixed),
            vec, vec,
        ],
        out_specs=pl.BlockSpec((COMB_TILE, D_MODEL), row),
        scratch_shapes=[
            pltpu.VMEM((2, N_EXPERTS * COMB_WIN, D_MODEL), BF16),
            pltpu.VMEM((N_EXPERTS * COMB_WIN, D_MODEL), BF16),
            pltpu.VMEM((COMB_TILE, D_MODEL), F32),
            pltpu.SemaphoreType.DMA((2,)),
            pltpu.SemaphoreType.DMA((1,)),
        ],
    )
    return pl.pallas_call(
        functools.partial(_combine_kernel, n_tiles=n_tiles),
        grid_spec=grid_spec,
        out_shape=jax.ShapeDtypeStruct((tokens, D_MODEL), F32),
        compiler_params=_cparams(("arbitrary",)),
        name="combine",
    )(win_start, x1, p2d, lp_tok, aff, ye, wpg, bpg, wple, ln_g, ln_b)


def _prep_weights(ln_in_g, ln_in_b, w_in, q_norm_w, k_norm_w, conv_w, conv_b, dt_bias, a_log,
                  d_skip, ssd_norm_w, w_out, ln1_g, ln1_b, w_router, w_e_gate, w_e_up, w_e_down,
                  w_ple_gate, b_ple_gate, w_ple, ln2_g, ln2_b):
    l = 0
    vec = lambda a: a.reshape(1, -1)
    w_in_pad = jnp.zeros((D_MODEL, IN_COLS_PAD), F32).at[:, :IN_COLS].set(w_in[l]).astype(BF16)
    wr = jnp.zeros((D_MODEL, LANES), F32).at[:, :N_EXPERTS].set(w_router[l])
    wr_hi, wr_lo = _split_bf16(wr)
    return dict(
        ln_in_g=vec(ln_in_g), ln_in_b=vec(ln_in_b), w_in=w_in_pad,
        qw=vec(jnp.tile(q_norm_w[l], N_Q_HEADS)), kw=vec(jnp.tile(k_norm_w[l], N_KV_HEADS)),
        conv_w=conv_w[l], conv_b=conv_b[l], dt_bias=dt_bias[l], a_log=a_log[l], d_skip=d_skip[l],
        norm_w=vec(ssd_norm_w[l]), w_out=w_out[l].astype(BF16), ln1_g=vec(ln1_g[l]), ln1_b=vec(ln1_b[l]),
        wr_hi=wr_hi, wr_lo=wr_lo,
        wg=w_e_gate[l].astype(BF16), wu=w_e_up[l].astype(BF16), wd=w_e_down[l].astype(BF16),
        wpg=w_ple_gate[l].astype(BF16), bpg=vec(b_ple_gate[l]), wple=w_ple[l].astype(BF16),
        ln2_g=vec(ln2_g[l]), ln2_b=vec(ln2_b[l]),
    )


def _trunk(x, p, w):
    batch, seq, _ = x.shape
    tokens = batch * seq
    x2d = x.reshape(tokens, D_MODEL)
    q, k, v, z, xc, dt2 = _inproj_call(x2d, seq, w["ln_in_g"], w["ln_in_b"], w["w_in"], w["qw"], w["kw"],
                                       w["conv_w"], w["conv_b"])
    attn = _attn_call(q, k, v, batch, seq)
    y2 = _ssd_call(xc, dt2, batch, seq, w["dt_bias"], w["a_log"], w["d_skip"])
    x1, aff = _mix_call(x2d, w["ln_in_g"], w["ln_in_b"], attn, y2, z, w["norm_w"], w["w_out"],
                        w["ln1_g"], w["ln1_b"], w["wr_hi"], w["wr_lo"])
    idx, lp_tok, win_start, cap = _select_call(aff)
    ye = _ffn_call(idx, x1, w["wg"], w["wu"], w["wd"], cap)
    out = _combine_call(win_start, x1, p[0].reshape(tokens, PLE_DIM), lp_tok, aff, ye,
                        w["wpg"], w["bpg"], w["wple"], w["ln2_g"], w["ln2_b"])
    return out.reshape(batch, seq, D_MODEL)


def kernel(x_prompt, x_sample, p_prompt, p_sample, ln_in_g, ln_in_b, w_in, q_norm_w, k_norm_w, conv_w, conv_b, dt_bias, a_log, d_skip, ssd_norm_w, w_out, ln1_g, ln1_b, w_router, w_e_gate, w_e_up, w_e_down, w_ple_gate, b_ple_gate, w_ple, ln2_g, ln2_b):
    w = _prep_weights(ln_in_g, ln_in_b, w_in, q_norm_w, k_norm_w, conv_w, conv_b, dt_bias, a_log,
                      d_skip, ssd_norm_w, w_out, ln1_g, ln1_b, w_router, w_e_gate, w_e_up, w_e_down,
                      w_ple_gate, b_ple_gate, w_ple, ln2_g, ln2_b)
    return (_trunk(x_prompt, p_prompt, w), _trunk(x_sample, p_sample, w))
```

```python
import functools
import math

import jax
import jax.numpy as jnp
from jax import lax
from jax.experimental import pallas as pl
from jax.experimental.pallas import tpu as pltpu

F32 = jnp.float32
BF16 = jnp.bfloat16
I32 = jnp.int32

D_MODEL = 1024
DEPTH = 1
GRID_W = 64
N_Q_HEADS = 8
N_KV_HEADS = 2
HEAD_DIM = 64
Q_PER_KV = N_Q_HEADS // N_KV_HEADS
ATTN_W = N_Q_HEADS * HEAD_DIM
KV_W = N_KV_HEADS * HEAD_DIM
ROPE_THETA = 10000.0
SSD_HEADS = 8
SSD_HEAD_DIM = 64
D_INNER = SSD_HEADS * SSD_HEAD_DIM
SSD_GROUPS = 2
HEADS_PER_GROUP = SSD_HEADS // SSD_GROUPS
D_STATE = 128
D_CONV = 5
CHUNK = 128
CONV_CH = D_INNER + 2 * SSD_GROUPS * D_STATE
K0 = ATTN_W
V0 = K0 + KV_W
Z0 = V0 + KV_W
XBC0 = Z0 + D_INNER
DT0 = XBC0 + CONV_CH
IN_COLS = DT0 + 2 * SSD_HEADS
N_EXPERTS = 16
CAPACITY_FACTOR = 2
D_EXPERT = 1024
PLE_DIM = 256
DN_ALPHA = (2.0 * DEPTH) ** 0.25
NORM_EPS = 1e-6
LN_EPS = 1e-5

LANES = 128
BF16_SUBLANES = 16
IN_COLS_PAD = DT0 + LANES
TOKEN_TILE = 512
ROW_GROUPS = 2
HALO = 8
SSD_BLOCK = 4
ATTN_Q_TILE = 256
ATTN_V_ROWS = HEAD_DIM + BF16_SUBLANES
FFN_ROWS = 256
COMB_TILE = 256
COMB_WIN = 64
COMB_ROUNDS = -(-(COMB_TILE + BF16_SUBLANES - 1) // COMB_WIN)
COMB_SPAN = COMB_ROUNDS * COMB_WIN
SEL_SLOT_CHUNK = 1024
SEL_EXPERTS = 2
VMEM_LIMIT = 56 * 1024 * 1024


def _cparams(sem):
    return pltpu.CompilerParams(dimension_semantics=sem, vmem_limit_bytes=VMEM_LIMIT)


def _layer_norm_rows(x, g, b):
    xc = x - jnp.mean(x, axis=-1, keepdims=True)
    return xc * lax.rsqrt(jnp.mean(xc * xc, axis=-1, keepdims=True) + LN_EPS) * g + b


def _sigmoid(x):
    return 1.0 / (1.0 + jnp.exp(-x))


def _dot(a, b):
    return jnp.dot(a, b, preferred_element_type=F32)


def _dot_nt(a, b):
    return lax.dot_general(a, b, (((1,), (1,)), ((), ())), preferred_element_type=F32)


def _split_bf16(x):
    hi = x.astype(BF16)
    lo = (x - hi.astype(F32)).astype(BF16)
    return hi, lo


def _split3_bf16(x):
    hi = x.astype(BF16)
    rest = x - hi.astype(F32)
    mid = rest.astype(BF16)
    lo = (rest - mid.astype(F32)).astype(BF16)
    return hi, mid, lo


def _rope_swap(x):
    outs = []
    for c in range(x.shape[-1] // LANES):
        xc = x[:, c * LANES:(c + 1) * LANES]
        lane = lax.broadcasted_iota(I32, xc.shape, 1)
        up = pltpu.roll(xc, LANES - 16, axis=1)
        dn = pltpu.roll(xc, 16, axis=1)
        outs.append(jnp.where((lane % 32) < 16, up, dn))
    return outs[0] if len(outs) == 1 else jnp.concatenate(outs, axis=-1)


def _head_rms(x, w):
    width = x.shape[-1]
    r = lax.broadcasted_iota(I32, (width, width), 0) // HEAD_DIM
    c = lax.broadcasted_iota(I32, (width, width), 1) // HEAD_DIM
    blockdiag = jnp.where(r == c, 1.0, 0.0).astype(BF16)
    hi, lo = _split_bf16(x * x)
    ssq = _dot(hi, blockdiag) + _dot(lo, blockdiag)
    return x * lax.rsqrt(ssq * (1.0 / HEAD_DIM) + NORM_EPS) * w


def _inproj_kernel(x_ref, xprev_ref, xnext_ref, g_ref, b_ref, w_ref, qw_ref, kw_ref, cos_ref, sin_ref,
                   cw_ref, cb_ref, q_ref, k_ref, v_ref, z_ref, xc_ref, dt_ref, pad_ref, *, tiles_per_seq):
    tm = x_ref.shape[0]
    sub = tm // ROW_GROUPS

    pos = pl.program_id(0) % tiles_per_seq
    has_prev = (pos > 0).astype(F32)
    has_next = (pos < tiles_per_seq - 1).astype(F32)
    edge = jnp.concatenate([xprev_ref[...], xnext_ref[...]], axis=0)
    edge = _dot(_layer_norm_rows(edge, g_ref[...], b_ref[...]).astype(BF16), w_ref[:, XBC0:DT0])
    pad_ref[0:HALO, :] = edge[:HALO] * has_prev
    pad_ref[HALO + tm:, :] = edge[HALO:] * has_next

    for grp in range(ROW_GROUPS):
        rs = slice(grp * sub, (grp + 1) * sub)
        xln = _layer_norm_rows(x_ref[rs, :], g_ref[...], b_ref[...])
        proj = _dot(xln.astype(BF16), w_ref[...])
        cos = cos_ref[rs, :]
        sin = sin_ref[rs, :]
        q = _head_rms(proj[:, :K0], qw_ref[...])
        cos_q = jnp.concatenate([cos] * (ATTN_W // LANES), axis=-1)
        sin_q = jnp.concatenate([sin] * (ATTN_W // LANES), axis=-1)
        q = (q * cos_q + _rope_swap(q) * sin_q) * (HEAD_DIM ** -0.5)
        q_ref[rs, :] = q.astype(BF16)
        k = _head_rms(proj[:, K0:V0], kw_ref[...])
        k = (k * cos + _rope_swap(k) * sin).astype(BF16)
        v = proj[:, V0:Z0]
        ones_col = jnp.where(lax.broadcasted_iota(I32, (sub, LANES - HEAD_DIM), 1) == 0, 1.0, 0.0)
        for g in range(N_KV_HEADS):
            k_ref[g, rs, :] = k[:, g * HEAD_DIM:(g + 1) * HEAD_DIM]
            v_ext = jnp.concatenate([v[:, g * HEAD_DIM:(g + 1) * HEAD_DIM], ones_col], axis=-1)
            v_ref[g, :, rs] = v_ext.T.astype(BF16)
        z_ref[rs, :] = proj[:, Z0:XBC0]
        pad_ref[HALO + grp * sub:HALO + (grp + 1) * sub, :] = proj[:, XBC0:DT0]
        dt = proj[:, DT0:DT0 + LANES]
        for d in range(2):
            dt_ref[d, rs, :] = dt[:, d * SSD_HEADS:(d + 1) * SSD_HEADS]

    cw = cw_ref[...]
    for grp in range(ROW_GROUPS):
        window = pad_ref[grp * sub:grp * sub + sub + 2 * HALO, :]
        acc = jnp.broadcast_to(cb_ref[...], (sub, CONV_CH))
        for tap in range(D_CONV):
            shift = (D_CONV // 2 - tap) % (sub + 2 * HALO)
            shifted = window if shift == 0 else pltpu.roll(window, shift, axis=0)
            acc = acc + shifted[HALO:HALO + sub, :] * cw[tap:tap + 1, :]
        xc_ref[grp * sub:(grp + 1) * sub, :] = acc * _sigmoid(acc)


def _rope_tables(seq):
    rows = seq // GRID_W
    row_id = jnp.broadcast_to(jnp.arange(rows)[:, None], (rows, GRID_W)).reshape(seq)
    col_id = jnp.broadcast_to(jnp.arange(GRID_W)[None, :], (rows, GRID_W)).reshape(seq)
    pos = jnp.stack([row_id, col_id], axis=-1).astype(F32)
    n_freq = HEAD_DIM // 4
    inv_freq = ROPE_THETA ** (-jnp.arange(n_freq, dtype=F32) / n_freq)
    ang = pos[:, :, None] * inv_freq
    cos = jnp.broadcast_to(jnp.cos(ang)[:, :, None, :], (seq, 2, 2, n_freq)).reshape(seq, HEAD_DIM)
    sin = jnp.broadcast_to(jnp.sin(ang)[:, :, None, :], (seq, 2, 2, n_freq))
    sin = (sin * jnp.array([-1.0, 1.0], F32)[None, None, :, None]).reshape(seq, HEAD_DIM)
    reps = LANES // HEAD_DIM
    return jnp.tile(cos, (1, reps)), jnp.tile(sin, (1, reps))


def _inproj_call(x2d, seq, ln_g, ln_b, w_in_pad, qw, kw, conv_w, conv_b):
    tokens = x2d.shape[0]
    tm = TOKEN_TILE
    nt = tokens // tm
    tiles_per_seq = seq // tm
    halo_per_tile = tm // HALO
    n_halo = tokens // HALO
    cos, sin = _rope_tables(seq)
    cw = jnp.zeros((8, CONV_CH), F32).at[:D_CONV].set(conv_w)
    cb = conv_b.reshape(1, CONV_CH)
    row = lambda i: (i, 0)
    fixed = lambda i: (0, 0)
    return pl.pallas_call(
        functools.partial(_inproj_kernel, tiles_per_seq=tiles_per_seq),
        grid=(nt,),
        in_specs=[
            pl.BlockSpec((tm, D_MODEL), row),
            pl.BlockSpec((HALO, D_MODEL), lambda i: (jnp.maximum(i * halo_per_tile - 1, 0), 0)),
            pl.BlockSpec((HALO, D_MODEL), lambda i: (jnp.minimum((i + 1) * halo_per_tile, n_halo - 1), 0)),
            pl.BlockSpec((1, D_MODEL), fixed),
            pl.BlockSpec((1, D_MODEL), fixed),
            pl.BlockSpec((D_MODEL, IN_COLS_PAD), fixed),
            pl.BlockSpec((1, ATTN_W), fixed),
            pl.BlockSpec((1, KV_W), fixed),
            pl.BlockSpec((tm, LANES), lambda i: (i % tiles_per_seq, 0)),
            pl.BlockSpec((tm, LANES), lambda i: (i % tiles_per_seq, 0)),
            pl.BlockSpec((8, CONV_CH), fixed),
            pl.BlockSpec((1, CONV_CH), fixed),
        ],
        out_specs=[
            pl.BlockSpec((tm, ATTN_W), row),
            pl.BlockSpec((N_KV_HEADS, tm, HEAD_DIM), lambda i: (0, i, 0)),
            pl.BlockSpec((N_KV_HEADS, LANES, tm), lambda i: (0, 0, i)),
            pl.BlockSpec((tm, D_INNER), row),
            pl.BlockSpec((tm, CONV_CH), row),
            pl.BlockSpec((2, tm, SSD_HEADS), lambda i: (0, i, 0)),
        ],
        out_shape=[
            jax.ShapeDtypeStruct((tokens, ATTN_W), BF16),
            jax.ShapeDtypeStruct((N_KV_HEADS, tokens, HEAD_DIM), BF16),
            jax.ShapeDtypeStruct((N_KV_HEADS, LANES, tokens), BF16),
            jax.ShapeDtypeStruct((tokens, D_INNER), F32),
            jax.ShapeDtypeStruct((tokens, CONV_CH), F32),
            jax.ShapeDtypeStruct((2, tokens, SSD_HEADS), F32),
        ],
        scratch_shapes=[pltpu.VMEM((tm + 2 * HALO, CONV_CH), F32)],
        compiler_params=_cparams(("parallel",)),
        name="inproj",
    )(x2d, x2d, x2d, ln_g, ln_b, w_in_pad, qw, kw, cos, sin, cw, cb)


def _attn_kernel(q_ref, k_ref, vt_ref, o_ref):
    k = k_ref[...]
    vt = vt_ref[0:ATTN_V_ROWS, :]
    q = q_ref[...]
    tq = q.shape[0]
    qs = jnp.concatenate([q[:, h * HEAD_DIM:(h + 1) * HEAD_DIM] for h in range(Q_PER_KV)], axis=0)
    s_all = _dot_nt(qs, k)
    outs = []
    for h in range(Q_PER_KV):
        s = s_all[h * tq:(h + 1) * tq]
        m = jnp.max(s, axis=-1, keepdims=True)
        p = jnp.exp((s - m).astype(BF16))
        o = _dot_nt(vt, p).T
        outs.append(o[:, :HEAD_DIM] / o[:, HEAD_DIM:HEAD_DIM + 1])
    o_ref[...] = jnp.concatenate(outs, axis=-1).astype(BF16)


def _attn_call(q, k, v, batch, seq):
    tokens = batch * seq
    tq = ATTN_Q_TILE
    nq = seq // tq
    width = Q_PER_KV * HEAD_DIM
    return pl.pallas_call(
        _attn_kernel,
        grid=(batch, N_KV_HEADS, nq),
        in_specs=[
            pl.BlockSpec((tq, width), lambda b, g, i: (b * nq + i, g)),
            pl.BlockSpec((None, seq, HEAD_DIM), lambda b, g, i: (g, b, 0)),
            pl.BlockSpec((None, LANES, seq), lambda b, g, i: (g, 0, b)),
        ],
        out_specs=pl.BlockSpec((tq, width), lambda b, g, i: (b * nq + i, g)),
        out_shape=jax.ShapeDtypeStruct((tokens, ATTN_W), BF16),
        compiler_params=_cparams(("parallel", "parallel", "parallel")),
        name="attn",
    )(q, k, v)


def _ssd_kernel(xc_ref, dt_ref, dtt_ref, dtb_ref, dtbt_ref, alog_ref, alogt_ref, dskip_ref,
                y_ref, state_ref):
    fwd = pl.program_id(1) == 0

    @pl.when(pl.program_id(2) == 0)
    def _():
        state_ref[...] = jnp.zeros_like(state_ref)

    for j in range(SSD_BLOCK):
        sub = jnp.where(fwd, j, SSD_BLOCK - 1 - j)
        _ssd_chunk(pl.multiple_of(sub * CHUNK, CHUNK), fwd, xc_ref, dt_ref, dtt_ref, dtb_ref, dtbt_ref,
                   alog_ref, alogt_ref, dskip_ref, y_ref, state_ref)


def _ssd_chunk(row0, fwd, xc_ref, dt_ref, dtt_ref, dtb_ref, dtbt_ref, alog_ref, alogt_ref, dskip_ref,
               y_ref, state_ref):
    rows = pl.ds(row0, CHUNK)
    xc = xc_ref[rows, :]
    xs = xc[:, :D_INNER]
    b_all = xc[:, D_INNER:D_INNER + SSD_GROUPS * D_STATE]
    c_all = xc[:, D_INNER + SSD_GROUPS * D_STATE:]

    dt = jax.nn.softplus(dt_ref[rows, :] + dtb_ref[...])
    a = dt * -jnp.exp(alog_ref[...])
    a_row = jax.nn.softplus(dtt_ref[:, rows] + dtbt_ref[...]) * -jnp.exp(alogt_ref[...])

    ri = lax.broadcasted_iota(I32, (CHUNK, CHUNK), 0)
    ci = lax.broadcasted_iota(I32, (CHUNK, CHUNK), 1)
    ahead = jnp.where(fwd, ri - ci, ci - ri)
    causal = ahead >= 0
    tri = jnp.where(causal, 1.0, 0.0).astype(BF16)
    tri_t = jnp.where(ahead <= 0, 1.0, 0.0).astype(BF16)
    cum = sum(_dot(tri, piece) for piece in _split3_bf16(a))
    cum_row = sum(_dot(piece, tri_t) for piece in _split3_bf16(a_row))
    total = jnp.sum(a, axis=0, keepdims=True)

    er = lax.broadcasted_iota(I32, (SSD_HEADS, D_INNER), 0)
    ec = lax.broadcasted_iota(I32, (SSD_HEADS, D_INNER), 1) // SSD_HEAD_DIM
    expand = jnp.where(er == ec, 1.0, 0.0).astype(BF16)
    stacked = jnp.concatenate([dt, jnp.exp(total - cum), jnp.exp(cum)], axis=0)
    stacked_e = sum(_dot(piece, expand) for piece in _split_bf16(stacked))
    dt_e = stacked_e[:CHUNK]
    dec_state_e = stacked_e[CHUNK:2 * CHUNK]
    dec_out_e = stacked_e[2 * CHUNK:]
    dec_chunk_e = sum(_dot(piece, expand) for piece in _split_bf16(jnp.exp(total)))

    xdt = xs * dt_e
    xdt_bf = xdt.astype(BF16)
    xdec_bf = (xdt * dec_state_e).astype(BF16)

    ys = []
    for g in range(SSD_GROUPS):
        bg = b_all[:, g * D_STATE:(g + 1) * D_STATE]
        cg = c_all[:, g * D_STATE:(g + 1) * D_STATE].astype(BF16)
        cb = _dot_nt(cg, bg.astype(BF16))
        lo = g * HEADS_PER_GROUP * SSD_HEAD_DIM
        hi = lo + HEADS_PER_GROUP * SSD_HEAD_DIM
        prev = state_ref[g]
        y_off = _dot(cg, prev.astype(BF16)) * dec_out_e[:, lo:hi]
        y_heads = []
        for r in range(HEADS_PER_GROUP):
            h = g * HEADS_PER_GROUP + r
            seg = cum[:, h:h + 1] - cum_row[h:h + 1, :]
            m = jnp.where(causal, cb * jnp.exp(jnp.where(causal, seg, 0.0)), 0.0)
            y_heads.append(_dot(m.astype(BF16), xdt_bf[:, h * SSD_HEAD_DIM:(h + 1) * SSD_HEAD_DIM]))
        ys.append(jnp.concatenate(y_heads, axis=-1) + y_off)
        st = _dot(bg.T.astype(BF16), xdec_bf[:, lo:hi])
        state_ref[g] = prev * dec_chunk_e[:, lo:hi] + st
    y = jnp.concatenate(ys, axis=-1)
    y_ref[rows, :] = y + jnp.where(fwd, 1.0, 0.0) * (dskip_ref[...] * xs)


def _ssd_call(xc, dt2, batch, seq, dt_bias, a_log, d_skip):
    tokens = batch * seq
    rows = SSD_BLOCK * CHUNK
    nc = seq // rows
    dtt = dt2.reshape(2, batch, seq, SSD_HEADS).transpose(1, 0, 3, 2).reshape(batch * 2 * SSD_HEADS, seq)
    dtb = dt_bias.reshape(2, 1, SSD_HEADS)
    dtbt = dt_bias.reshape(2, SSD_HEADS, 1)
    na = a_log.reshape(2, 1, SSD_HEADS)
    nat = a_log.reshape(2, SSD_HEADS, 1)
    dskip_e = jnp.repeat(d_skip, SSD_HEAD_DIM).reshape(1, D_INNER)

    def ceff(d, c):
        return jnp.where(d == 0, c, nc - 1 - c)

    def main_idx(b, d, c):
        return (b * nc + ceff(d, c), 0)

    sel_d = lambda b, d, c: (d, 0, 0)
    fixed = lambda b, d, c: (0, 0)
    return pl.pallas_call(
        _ssd_kernel,
        grid=(batch, 2, nc),
        in_specs=[
            pl.BlockSpec((rows, CONV_CH), main_idx),
            pl.BlockSpec((None, rows, SSD_HEADS), lambda b, d, c: (d, b * nc + ceff(d, c), 0)),
            pl.BlockSpec((SSD_HEADS, rows), lambda b, d, c: (b * 2 + d, ceff(d, c))),
            pl.BlockSpec((None, 1, SSD_HEADS), sel_d),
            pl.BlockSpec((None, SSD_HEADS, 1), sel_d),
            pl.BlockSpec((None, 1, SSD_HEADS), sel_d),
            pl.BlockSpec((None, SSD_HEADS, 1), sel_d),
            pl.BlockSpec((1, D_INNER), fixed),
        ],
        out_specs=pl.BlockSpec((None, rows, D_INNER), lambda b, d, c: (d, b * nc + ceff(d, c), 0)),
        out_shape=jax.ShapeDtypeStruct((2, tokens, D_INNER), F32),
        scratch_shapes=[
            pltpu.VMEM((SSD_GROUPS, D_STATE, HEADS_PER_GROUP * SSD_HEAD_DIM), F32),
        ],
        compiler_params=_cparams(("parallel", "arbitrary", "arbitrary")),
        name="ssd",
    )(xc, dt2, dtt, dtb, dtbt, na, nat, dskip_e)


def _mix_kernel(x_ref, ing_ref, inb_ref, attn_ref, yf_ref, yb_ref, z_ref, nw_ref, wo_ref,
                g_ref, b_ref, wrh_ref, wrl_ref, x1_ref, aff_ref):
    sub = x_ref.shape[0] // ROW_GROUPS
    for grp in range(ROW_GROUPS):
        rs = slice(grp * sub, (grp + 1) * sub)
        xln = _layer_norm_rows(x_ref[rs, :], ing_ref[...], inb_ref[...])
        z = z_ref[rs, :]
        y = (yf_ref[rs, :] + yb_ref[rs, :]) * (z * _sigmoid(z))
        gw = D_INNER // SSD_GROUPS
        parts = []
        for g in range(SSD_GROUPS):
            yg = y[:, g * gw:(g + 1) * gw]
            parts.append(yg * lax.rsqrt(jnp.mean(yg * yg, axis=-1, keepdims=True) + NORM_EPS))
        ssd = (jnp.concatenate(parts, axis=-1) * nw_ref[...]).astype(BF16)
        mix = _dot(jnp.concatenate([attn_ref[rs, :], ssd], axis=-1), wo_ref[...])
        x1 = _layer_norm_rows(DN_ALPHA * xln + mix, g_ref[...], b_ref[...])
        x1_ref[rs, :] = x1
        hi, lo = _split_bf16(x1)
        wrh = wrh_ref[...]
        logits = _dot(hi, wrh) + _dot(lo, wrh) + _dot(hi, wrl_ref[...])
        lane = lax.broadcasted_iota(I32, logits.shape, 1)
        logits = jnp.where(lane < N_EXPERTS, logits, -jnp.inf)
        e = jnp.exp(logits - jnp.max(logits, axis=-1, keepdims=True))
        aff = e / jnp.sum(e, axis=-1, keepdims=True)
        aff_ref[rs, :] = aff[:, :N_EXPERTS]


def _mix_call(x2d, ln_in_g, ln_in_b, attn, y2, z, norm_w, w_out, ln_g, ln_b, wr_hi, wr_lo):
    tokens = x2d.shape[0]
    tm = TOKEN_TILE
    row = lambda i: (i, 0)
    fixed = lambda i: (0, 0)
    vec = pl.BlockSpec((1, D_MODEL), fixed)
    return pl.pallas_call(
        _mix_kernel,
        grid=(tokens // tm,),
        in_specs=[
            pl.BlockSpec((tm, D_MODEL), row), vec, vec,
            pl.BlockSpec((tm, ATTN_W), row),
            pl.BlockSpec((None, tm, D_INNER), lambda i: (0, i, 0)),
            pl.BlockSpec((None, tm, D_INNER), lambda i: (1, i, 0)),
            pl.BlockSpec((tm, D_INNER), row),
            pl.BlockSpec((1, D_INNER), fixed),
            pl.BlockSpec((D_MODEL, D_MODEL), fixed),
            vec, vec,
            pl.BlockSpec((D_MODEL, LANES), fixed),
            pl.BlockSpec((D_MODEL, LANES), fixed),
        ],
        out_specs=[
            pl.BlockSpec((tm, D_MODEL), row),
            pl.BlockSpec((tm, N_EXPERTS), row),
        ],
        out_shape=[
            jax.ShapeDtypeStruct((tokens, D_MODEL), F32),
            jax.ShapeDtypeStruct((tokens, N_EXPERTS), F32),
        ],
        compiler_params=_cparams(("parallel",)),
        name="mix",
    )(x2d, ln_in_g, ln_in_b, attn, y2, y2, z, norm_w, w_out, ln_g, ln_b, wr_hi, wr_lo)


def _tile_cumsum(mask_bf, upper, strict_lower):
    wloc = _dot(mask_bf, upper)
    tot = jnp.broadcast_to(wloc[:, LANES - 1:LANES], wloc.shape).astype(BF16)
    offs = _dot(strict_lower, tot)
    return wloc, offs


def _select_kernel(aff_ref, idx_ref, lp_ref, win_ref, *, cap):
    for e in range(SEL_EXPERTS):
        _select_expert(aff_ref.at[e], idx_ref.at[e], lp_ref.at[e], win_ref.at[e], cap)


def _select_expert(aff_ref, idx_ref, lp_ref, win_ref, cap):
    nt = aff_ref.shape[0]
    aff = aff_ref[...]

    def enough(cand_bits):
        cnt = jnp.sum(jnp.where(aff >= pltpu.bitcast(cand_bits, F32), 1.0, 0.0), keepdims=True)
        return cnt >= cap

    t_bits = jnp.where(enough(jnp.full((1, 1), 1 << 30, I32)), 1 << 30, 0).astype(I32)
    for bit in range(29, 0, -2):
        hi, lo = 1 << bit, 1 << (bit - 1)
        c1, c2, c3 = t_bits | lo, t_bits | hi, t_bits | hi | lo
        t_bits = jnp.where(enough(c3), c3, jnp.where(enough(c2), c2, jnp.where(enough(c1), c1, t_bits)))
    t = pltpu.bitcast(t_bits, F32)
    gt = aff > t
    eq = aff == t
    need = cap - jnp.sum(jnp.where(gt, 1.0, 0.0), keepdims=True)

    ri = lax.broadcasted_iota(I32, (LANES, LANES), 0)
    ci = lax.broadcasted_iota(I32, (LANES, LANES), 1)
    upper = jnp.where(ri <= ci, 1.0, 0.0).astype(BF16)
    rt = lax.broadcasted_iota(I32, (nt, nt), 0)
    ct = lax.broadcasted_iota(I32, (nt, nt), 1)
    strict_lower = jnp.where(ct < rt, 1.0, 0.0).astype(BF16)

    eq_loc, eq_offs = _tile_cumsum(jnp.where(eq, 1.0, 0.0).astype(BF16), upper, strict_lower)
    sel = gt | (eq & (eq_loc + eq_offs <= need))
    sel_bf = jnp.where(sel, 1.0, 0.0).astype(BF16)
    wloc, offs = _tile_cumsum(sel_bf, upper, strict_lower)

    sel_tiles = COMB_TILE // LANES
    first_lower = jnp.where(ct < (rt // sel_tiles) * sel_tiles, 1.0, 0.0).astype(BF16)
    offs_first = _dot(first_lower, jnp.broadcast_to(wloc[:, LANES - 1:LANES], wloc.shape).astype(BF16))
    win = jnp.minimum(jnp.floor(offs_first * (1.0 / BF16_SUBLANES)) * BF16_SUBLANES, float(cap - COMB_SPAN))
    win_ref[...] = win
    lp_ref[...] = jnp.where(sel, wloc + offs - 1.0 - win, -1.0)

    tot = jnp.broadcast_to(wloc[:, LANES - 1:LANES], wloc.shape)
    tot_row = _dot_nt(jnp.ones((8, LANES), BF16), sel_bf)
    upper_t = jnp.where(rt <= ct, 1.0, 0.0).astype(BF16)
    incl_row = _dot(tot_row.astype(BF16), upper_t)[0:1, :]
    next_tile = jnp.where(ct == rt + 1, 1.0, 0.0).astype(BF16)
    wloc_step = _dot(next_tile, wloc.astype(BF16)) - wloc
    lane_id = lax.broadcasted_iota(I32, (nt, LANES), 1)
    side = jnp.where(lane_id == 0, 1.0, jnp.where(lane_id == 1, tot, 0.0))
    steps = jnp.concatenate([wloc_step, side], axis=-1).astype(BF16)
    first = jnp.concatenate([wloc[0:1, :], jnp.zeros((1, LANES), F32)], axis=-1)
    for s0 in range(0, cap, SEL_SLOT_CHUNK):
        n = min(SEL_SLOT_CHUNK, cap - s0)
        slot = (lax.broadcasted_iota(I32, (n, 1), 0) + s0).astype(F32)
        before = jnp.where(incl_row <= slot, 1.0, 0.0).astype(BF16)
        got = _dot(before, steps) + first
        tile = got[:, LANES:LANES + 1]
        base = got[:, LANES + 1:LANES + 2]
        lane = jnp.sum(jnp.where(got[:, :LANES] <= slot - base, 1.0, 0.0), axis=-1, keepdims=True)
        tok = jnp.broadcast_to(tile * LANES + lane, (n, LANES))
        idx_ref[:, s0:s0 + n] = tok.T[0:1, :].astype(I32)


def _select_call(aff):
    tokens = aff.shape[0]
    cap = CAPACITY_FACTOR * tokens // N_EXPERTS
    nt = tokens // LANES
    aff_t = aff.T.reshape(N_EXPERTS, nt, LANES)
    blk = pl.BlockSpec((SEL_EXPERTS, nt, LANES), lambda e: (e, 0, 0))
    idx, lp, win = pl.pallas_call(
        functools.partial(_select_kernel, cap=cap),
        grid=(N_EXPERTS // SEL_EXPERTS,),
        in_specs=[blk],
        out_specs=[pl.BlockSpec((SEL_EXPERTS, 1, cap), lambda e: (e, 0, 0)), blk, blk],
        out_shape=[
            jax.ShapeDtypeStruct((N_EXPERTS, 1, cap), I32),
            jax.ShapeDtypeStruct((N_EXPERTS, nt, LANES), F32),
            jax.ShapeDtypeStruct((N_EXPERTS, nt, LANES), F32),
        ],
        compiler_params=_cparams(("parallel",)),
        name="select",
    )(aff_t)
    lp_tok = lp.reshape(N_EXPERTS, tokens).T
    win_start = win[:, ::COMB_TILE // LANES, 0].astype(I32).reshape(-1)
    return idx.reshape(N_EXPERTS * cap), lp_tok, win_start, cap


def _ffn_kernel(idx_ref, x_hbm, wg_ref, wu_ref, wd_ref, ye_ref, land_a, land_b, xe_a, xe_b, sem_ref,
                *, cap):
    e = pl.program_id(0)
    c = pl.program_id(1)
    pairs = cap // (2 * FFN_ROWS)
    step = e * pairs + c
    n_steps = N_EXPERTS * pairs
    last = 2 * n_steps - 1
    land = (land_a, land_b)
    stage = (xe_a, xe_b)

    def row_copy(chunk, r, half):
        tok = idx_ref[chunk * FFN_ROWS + r]
        return pltpu.make_async_copy(x_hbm.at[pl.ds(tok, 1), :], land[half].at[pl.ds(r, 1), :],
                                     sem_ref.at[half])

    def start_chunk(chunk, half):
        for r in range(FFN_ROWS):
            row_copy(chunk, r, half).start()

    def wait_chunk(chunk, half):
        def body(r, carry):
            row_copy(chunk, r, half).wait()
            return carry
        lax.fori_loop(0, FFN_ROWS, body, 0, unroll=8)

    @pl.when(step == 0)
    def _():
        start_chunk(0, 0)
        start_chunk(1, 1)

    for half in range(2):
        wait_chunk(2 * step + half, half)
        stage[half][...] = land[half][...].astype(BF16)
        start_chunk(jnp.minimum(2 * step + 2 + half, last), half)
        xe = stage[half][...]
        gate = _dot(xe, wg_ref[...])
        up = _dot(xe, wu_ref[...])
        hid = (gate * _sigmoid(gate) * up).astype(BF16)
        ye_ref[half * FFN_ROWS:(half + 1) * FFN_ROWS, :] = _dot(hid, wd_ref[...]).astype(BF16)

    @pl.when(step == n_steps - 1)
    def _():
        wait_chunk(last, 0)
        wait_chunk(last, 1)


def _ffn_call(idx, x1, wg, wu, wd, cap):
    pairs = cap // (2 * FFN_ROWS)
    wspec = lambda d0, d1: pl.BlockSpec((None, d0, d1), lambda e, c, idx: (e, 0, 0))
    grid_spec = pltpu.PrefetchScalarGridSpec(
        num_scalar_prefetch=1,
        grid=(N_EXPERTS, pairs),
        in_specs=[
            pl.BlockSpec(memory_space=pl.ANY),
            wspec(D_MODEL, D_EXPERT), wspec(D_MODEL, D_EXPERT), wspec(D_EXPERT, D_MODEL),
        ],
        out_specs=pl.BlockSpec((None, 2 * FFN_ROWS, D_MODEL), lambda e, c, idx: (e, c, 0)),
        scratch_shapes=[
            pltpu.VMEM((FFN_ROWS, D_MODEL), F32),
            pltpu.VMEM((FFN_ROWS, D_MODEL), F32),
            pltpu.VMEM((FFN_ROWS, D_MODEL), BF16),
            pltpu.VMEM((FFN_ROWS, D_MODEL), BF16),
            pltpu.SemaphoreType.DMA((2,)),
        ],
    )
    return pl.pallas_call(
        functools.partial(_ffn_kernel, cap=cap),
        grid_spec=grid_spec,
        out_shape=jax.ShapeDtypeStruct((N_EXPERTS, cap, D_MODEL), BF16),
        compiler_params=_cparams(("arbitrary", "arbitrary")),
        name="ffn",
    )(idx, x1, wg, wu, wd)


def _combine_kernel(win_ref, x1_ref, p_ref, lp_ref, aff_ref, ye_hbm, wpg_ref, bpg_ref, wple_ref,
                    g_ref, b_ref, o_ref, buf_ref, xbuf_ref, acc_ref, sem_ref, xsem_ref, *, n_tiles):
    i = pl.program_id(0)
    slot = i % 2
    width = N_EXPERTS * COMB_WIN

    def win_copy(tile, e, rnd, dst, sem):
        start = pl.multiple_of(win_ref[e * n_tiles + tile] + rnd * COMB_WIN, BF16_SUBLANES)
        return pltpu.make_async_copy(ye_hbm.at[e, pl.ds(start, COMB_WIN), :],
                                     dst.at[pl.ds(e * COMB_WIN, COMB_WIN), :], sem)

    def start_tile(tile, buf_slot):
        for e in range(N_EXPERTS):
            win_copy(tile, e, 0, buf_ref.at[buf_slot], sem_ref.at[buf_slot]).start()

    @pl.when(i == 0)
    def _():
        start_tile(0, 0)

    @pl.when(i + 1 < n_tiles)
    def _():
        start_tile(i + 1, 1 - slot)

    x1 = x1_ref[...]
    ple = _sigmoid(_dot(x1.astype(BF16), wpg_ref[...]) + bpg_ref[...]) \
        * _dot(p_ref[...].astype(BF16), wple_ref[...])
    acc_ref[...] = DN_ALPHA * x1 + ple

    lp = lp_ref[...]
    rnd_of = jnp.floor(lp * (1.0 / COMB_WIN))
    row_of = lp - rnd_of * COMB_WIN
    er = lax.broadcasted_iota(I32, (N_EXPERTS, width), 0)
    ec = lax.broadcasted_iota(I32, (N_EXPERTS, width), 1) // COMB_WIN
    expand = jnp.where(er == ec, 1.0, 0.0).astype(BF16)
    lane_row = (lax.broadcasted_iota(I32, (COMB_TILE, width), 1) % COMB_WIN).astype(F32)
    hit = _dot(row_of.astype(BF16), expand) == lane_row
    rnd_e = _dot(rnd_of.astype(BF16), expand)
    g_hi, g_lo = _split_bf16(aff_ref[...])
    g_hi_e = jnp.where(hit, _dot(g_hi, expand), 0.0)
    g_lo_e = jnp.where(hit, _dot(g_lo, expand), 0.0)

    def add_round(rnd, rows):
        pick = rnd_e == rnd
        lhs = jnp.concatenate([jnp.where(pick, g_hi_e, 0.0).astype(BF16),
                               jnp.where(pick, g_lo_e, 0.0).astype(BF16)], axis=0)
        both = _dot(lhs, rows)
        acc_ref[...] += both[:COMB_TILE] + both[COMB_TILE:]

    for e in range(N_EXPERTS):
        win_copy(i, e, 0, buf_ref.at[slot], sem_ref.at[slot]).wait()
    add_round(0.0, buf_ref[slot])

    n_rounds = jnp.max(rnd_of).astype(I32) + 1
    for rnd in range(1, COMB_ROUNDS):
        @pl.when(rnd < n_rounds)
        def _():
            for e in range(N_EXPERTS):
                win_copy(i, e, rnd, xbuf_ref, xsem_ref.at[0]).start()
            for e in range(N_EXPERTS):
                win_copy(i, e, rnd, xbuf_ref, xsem_ref.at[0]).wait()
            add_round(float(rnd), xbuf_ref[...])

    o_ref[...] = _layer_norm_rows(acc_ref[...], g_ref[...], b_ref[...])


def _combine_call(win_start, x1, p2d, lp_tok, aff, ye, wpg, bpg, wple, ln_g, ln_b):
    tokens = x1.shape[0]
    n_tiles = tokens // COMB_TILE
    row = lambda i, w: (i, 0)
    fixed = lambda i, w: (0, 0)
    vec = pl.BlockSpec((1, D_MODEL), fixed)
    grid_spec = pltpu.PrefetchScalarGridSpec(
        num_scalar_prefetch=1,
        grid=(n_tiles,),
        in_specs=[
            pl.BlockSpec((COMB_TILE, D_MODEL), row),
            pl.BlockSpec((COMB_TILE, PLE_DIM), row),
            pl.BlockSpec((COMB_TILE, N_EXPERTS), row),
            pl.BlockSpec((COMB_TILE, N_EXPERTS), row),
            pl.BlockSpec(memory_space=pl.ANY),
            pl.BlockSpec((D_MODEL, D_MODEL), fixed),
            vec,
            pl.BlockSpec((PLE_DIM, D_MODEL), fixed),
            vec, vec,
        ],
        out_specs=pl.BlockSpec((COMB_TILE, D_MODEL), row),
        scratch_shapes=[
            pltpu.VMEM((2, N_EXPERTS * COMB_WIN, D_MODEL), BF16),
            pltpu.VMEM((N_EXPERTS * COMB_WIN, D_MODEL), BF16),
            pltpu.VMEM((COMB_TILE, D_MODEL), F32),
            pltpu.SemaphoreType.DMA((2,)),
            pltpu.SemaphoreType.DMA((1,)),
        ],
    )
    return pl.pallas_call(
        functools.partial(_combine_kernel, n_tiles=n_tiles),
        grid_spec=grid_spec,
        out_shape=jax.ShapeDtypeStruct((tokens, D_MODEL), F32),
        compiler_params=_cparams(("arbitrary",)),
        name="combine",
    )(win_start, x1, p2d, lp_tok, aff, ye, wpg, bpg, wple, ln_g, ln_b)


def _prep_weights(ln_in_g, ln_in_b, w_in, q_norm_w, k_norm_w, conv_w, conv_b, dt_bias, a_log,
                  d_skip, ssd_norm_w, w_out, ln1_g, ln1_b, w_router, w_e_gate, w_e_up, w_e_down,
                  w_ple_gate, b_ple_gate, w_ple, ln2_g, ln2_b):
    l = 0
    vec = lambda a: a.reshape(1, -1)
    w_in_pad = jnp.zeros((D_MODEL, IN_COLS_PAD), F32).at[:, :IN_COLS].set(w_in[l]).astype(BF16)
    wr = jnp.zeros((D_MODEL, LANES), F32).at[:, :N_EXPERTS].set(w_router[l])
    wr_hi, wr_lo = _split_bf16(wr)
    return dict(
        ln_in_g=vec(ln_in_g), ln_in_b=vec(ln_in_b), w_in=w_in_pad,
        qw=vec(jnp.tile(q_norm_w[l], N_Q_HEADS)), kw=vec(jnp.tile(k_norm_w[l], N_KV_HEADS)),
        conv_w=conv_w[l], conv_b=conv_b[l], dt_bias=dt_bias[l], a_log=a_log[l], d_skip=d_skip[l],
        norm_w=vec(ssd_norm_w[l]), w_out=w_out[l].astype(BF16), ln1_g=vec(ln1_g[l]), ln1_b=vec(ln1_b[l]),
        wr_hi=wr_hi, wr_lo=wr_lo,
        wg=w_e_gate[l].astype(BF16), wu=w_e_up[l].astype(BF16), wd=w_e_down[l].astype(BF16),
        wpg=w_ple_gate[l].astype(BF16), bpg=vec(b_ple_gate[l]), wple=w_ple[l].astype(BF16),
        ln2_g=vec(ln2_g[l]), ln2_b=vec(ln2_b[l]),
    )


def _trunk(x, p, w):
    batch, seq, _ = x.shape
    tokens = batch * seq
    x2d = x.reshape(tokens, D_MODEL)
    q, k, v, z, xc, dt2 = _inproj_call(x2d, seq, w["ln_in_g"], w["ln_in_b"], w["w_in"], w["qw"], w["kw"],
                                       w["conv_w"], w["conv_b"])
    attn = _attn_call(q, k, v, batch, seq)
    y2 = _ssd_call(xc, dt2, batch, seq, w["dt_bias"], w["a_log"], w["d_skip"])
    x1, aff = _mix_call(x2d, w["ln_in_g"], w["ln_in_b"], attn, y2, z, w["norm_w"], w["w_out"],
                        w["ln1_g"], w["ln1_b"], w["wr_hi"], w["wr_lo"])
    idx, lp_tok, win_start, cap = _select_call(aff)
    ye = _ffn_call(idx, x1, w["wg"], w["wu"], w["wd"], cap)
    out = _combine_call(win_start, x1, p[0].reshape(tokens, PLE_DIM), lp_tok, aff, ye,
                        w["wpg"], w["bpg"], w["wple"], w["ln2_g"], w["ln2_b"])
    return out.reshape(batch, seq, D_MODEL)


def kernel(x_prompt, x_sample, p_prompt, p_sample, ln_in_g, ln_in_b, w_in, q_norm_w, k_norm_w, conv_w, conv_b, dt_bias, a_log, d_skip, ssd_norm_w, w_out, ln1_g, ln1_b, w_router, w_e_gate, w_e_up, w_e_down, w_ple_gate, b_ple_gate, w_ple, ln2_g, ln2_b):
    w = _prep_weights(ln_in_g, ln_in_b, w_in, q_norm_w, k_norm_w, conv_w, conv_b, dt_bias, a_log,
                      d_skip, ssd_norm_w, w_out, ln1_g, ln1_b, w_router, w_e_gate, w_e_up, w_e_down,
                      w_ple_gate, b_ple_gate, w_ple, ln2_g, ln2_b)
    return (_trunk(x_prompt, p_prompt, w), _trunk(x_sample, p_sample, w))
```

```python
import functools
import math

import jax
import jax.numpy as jnp
from jax import lax
from jax.experimental import pallas as pl
from jax.experimental.pallas import tpu as pltpu

F32 = jnp.float32
BF16 = jnp.bfloat16
I32 = jnp.int32

D_MODEL = 1024
DEPTH = 1
GRID_W = 64
N_Q_HEADS = 8
N_KV_HEADS = 2
HEAD_DIM = 64
Q_PER_KV = N_Q_HEADS // N_KV_HEADS
ATTN_W = N_Q_HEADS * HEAD_DIM
KV_W = N_KV_HEADS * HEAD_DIM
ROPE_THETA = 10000.0
SSD_HEADS = 8
SSD_HEAD_DIM = 64
D_INNER = SSD_HEADS * SSD_HEAD_DIM
SSD_GROUPS = 2
HEADS_PER_GROUP = SSD_HEADS // SSD_GROUPS
D_STATE = 128
D_CONV = 5
CHUNK = 128
CONV_CH = D_INNER + 2 * SSD_GROUPS * D_STATE
K0 = ATTN_W
V0 = K0 + KV_W
Z0 = V0 + KV_W
XBC0 = Z0 + D_INNER
DT0 = XBC0 + CONV_CH
IN_COLS = DT0 + 2 * SSD_HEADS
N_EXPERTS = 16
CAPACITY_FACTOR = 2
D_EXPERT = 1024
PLE_DIM = 256
DN_ALPHA = (2.0 * DEPTH) ** 0.25
NORM_EPS = 1e-6
LN_EPS = 1e-5

LANES = 128
BF16_SUBLANES = 16
IN_COLS_PAD = DT0 + LANES
TOKEN_TILE = 1024
ROW_GROUPS = 4
HALO = 8
SSD_BLOCK = 8
ATTN_Q_TILE = 1024
ATTN_Q_SUB = 256
ATTN_V_ROWS = HEAD_DIM + BF16_SUBLANES
FFN_ROWS = 256
COMB_TILE = 256
COMB_WIN = 64
COMB_ROUNDS = -(-(COMB_TILE + BF16_SUBLANES - 1) // COMB_WIN)
COMB_SPAN = COMB_ROUNDS * COMB_WIN
SEL_SLOT_CHUNK = 1024
SEL_EXPERTS = 2
VMEM_LIMIT = 56 * 1024 * 1024


def _cparams(sem):
    return pltpu.CompilerParams(dimension_semantics=sem, vmem_limit_bytes=VMEM_LIMIT)


def _layer_norm_rows(x, g, b):
    xc = x - jnp.mean(x, axis=-1, keepdims=True)
    return xc * lax.rsqrt(jnp.mean(xc * xc, axis=-1, keepdims=True) + LN_EPS) * g + b


def _sigmoid(x):
    return 1.0 / (1.0 + jnp.exp(-x))


def _dot(a, b):
    return jnp.dot(a, b, preferred_element_type=F32)


def _dot_nt(a, b):
    return lax.dot_general(a, b, (((1,), (1,)), ((), ())), preferred_element_type=F32)


def _split_bf16(x):
    hi = x.astype(BF16)
    lo = (x - hi.astype(F32)).astype(BF16)
    return hi, lo


def _split3_bf16(x):
    hi = x.astype(BF16)
    rest = x - hi.astype(F32)
    mid = rest.astype(BF16)
    lo = (rest - mid.astype(F32)).astype(BF16)
    return hi, mid, lo


def _rope_swap(x):
    outs = []
    for c in range(x.shape[-1] // LANES):
        xc = x[:, c * LANES:(c + 1) * LANES]
        lane = lax.broadcasted_iota(I32, xc.shape, 1)
        up = pltpu.roll(xc, LANES - 16, axis=1)
        dn = pltpu.roll(xc, 16, axis=1)
        outs.append(jnp.where((lane % 32) < 16, up, dn))
    return outs[0] if len(outs) == 1 else jnp.concatenate(outs, axis=-1)


def _head_rms(x, w):
    width = x.shape[-1]
    r = lax.broadcasted_iota(I32, (width, width), 0) // HEAD_DIM
    c = lax.broadcasted_iota(I32, (width, width), 1) // HEAD_DIM
    blockdiag = jnp.where(r == c, 1.0, 0.0).astype(BF16)
    hi, lo = _split_bf16(x * x)
    ssq = _dot(hi, blockdiag) + _dot(lo, blockdiag)
    return x * lax.rsqrt(ssq * (1.0 / HEAD_DIM) + NORM_EPS) * w


def _inproj_kernel(x_ref, xprev_ref, xnext_ref, g_ref, b_ref, w_ref, qw_ref, kw_ref, cos_ref, sin_ref,
                   cw_ref, cb_ref, q_ref, k_ref, v_ref, z_ref, xc_ref, dt_ref, pad_ref, *, tiles_per_seq):
    tm = x_ref.shape[0]
    sub = tm // ROW_GROUPS

    pos = pl.program_id(0) % tiles_per_seq
    has_prev = (pos > 0).astype(F32)
    has_next = (pos < tiles_per_seq - 1).astype(F32)
    edge = jnp.concatenate([xprev_ref[...], xnext_ref[...]], axis=0)
    edge = _dot(_layer_norm_rows(edge, g_ref[...], b_ref[...]).astype(BF16), w_ref[:, XBC0:DT0])
    pad_ref[0:HALO, :] = edge[:HALO] * has_prev
    pad_ref[HALO + tm:, :] = edge[HALO:] * has_next

    for grp in range(ROW_GROUPS):
        rs = slice(grp * sub, (grp + 1) * sub)
        xln = _layer_norm_rows(x_ref[rs, :], g_ref[...], b_ref[...])
        proj = _dot(xln.astype(BF16), w_ref[...])
        cos = cos_ref[rs, :]
        sin = sin_ref[rs, :]
        q = _head_rms(proj[:, :K0], qw_ref[...])
        cos_q = jnp.concatenate([cos] * (ATTN_W // LANES), axis=-1)
        sin_q = jnp.concatenate([sin] * (ATTN_W // LANES), axis=-1)
        q = (q * cos_q + _rope_swap(q) * sin_q) * (HEAD_DIM ** -0.5)
        q_ref[rs, :] = q.astype(BF16)
        k = _head_rms(proj[:, K0:V0], kw_ref[...])
        k = (k * cos + _rope_swap(k) * sin).astype(BF16)
        v = proj[:, V0:Z0]
        ones_col = jnp.where(lax.broadcasted_iota(I32, (sub, LANES - HEAD_DIM), 1) == 0, 1.0, 0.0)
        for g in range(N_KV_HEADS):
            k_ref[g, rs, :] = k[:, g * HEAD_DIM:(g + 1) * HEAD_DIM]
            v_ext = jnp.concatenate([v[:, g * HEAD_DIM:(g + 1) * HEAD_DIM], ones_col], axis=-1)
            v_ref[g, :, rs] = v_ext.T.astype(BF16)
        z_ref[rs, :] = proj[:, Z0:XBC0]
        pad_ref[HALO + grp * sub:HALO + (grp + 1) * sub, :] = proj[:, XBC0:DT0]
        dt = proj[:, DT0:DT0 + LANES]
        for d in range(2):
            dt_ref[d, rs, :] = dt[:, d * SSD_HEADS:(d + 1) * SSD_HEADS]

    cw = cw_ref[...]
    for grp in range(ROW_GROUPS):
        window = pad_ref[grp * sub:grp * sub + sub + 2 * HALO, :]
        acc = jnp.broadcast_to(cb_ref[...], (sub, CONV_CH))
        for tap in range(D_CONV):
            shift = (D_CONV // 2 - tap) % (sub + 2 * HALO)
            shifted = window if shift == 0 else pltpu.roll(window, shift, axis=0)
            acc = acc + shifted[HALO:HALO + sub, :] * cw[tap:tap + 1, :]
        xc_ref[grp * sub:(grp + 1) * sub, :] = acc * _sigmoid(acc)


def _rope_tables(seq):
    rows = seq // GRID_W
    row_id = jnp.broadcast_to(jnp.arange(rows)[:, None], (rows, GRID_W)).reshape(seq)
    col_id = jnp.broadcast_to(jnp.arange(GRID_W)[None, :], (rows, GRID_W)).reshape(seq)
    pos = jnp.stack([row_id, col_id], axis=-1).astype(F32)
    n_freq = HEAD_DIM // 4
    inv_freq = ROPE_THETA ** (-jnp.arange(n_freq, dtype=F32) / n_freq)
    ang = pos[:, :, None] * inv_freq
    cos = jnp.broadcast_to(jnp.cos(ang)[:, :, None, :], (seq, 2, 2, n_freq)).reshape(seq, HEAD_DIM)
    sin = jnp.broadcast_to(jnp.sin(ang)[:, :, None, :], (seq, 2, 2, n_freq))
    sin = (sin * jnp.array([-1.0, 1.0], F32)[None, None, :, None]).reshape(seq, HEAD_DIM)
    reps = LANES // HEAD_DIM
    return jnp.tile(cos, (1, reps)), jnp.tile(sin, (1, reps))


def _inproj_call(x2d, seq, ln_g, ln_b, w_in_pad, qw, kw, conv_w, conv_b):
    tokens = x2d.shape[0]
    tm = TOKEN_TILE
    nt = tokens // tm
    tiles_per_seq = seq // tm
    halo_per_tile = tm // HALO
    n_halo = tokens // HALO
    cos, sin = _rope_tables(seq)
    cw = jnp.zeros((8, CONV_CH), F32).at[:D_CONV].set(conv_w)
    cb = conv_b.reshape(1, CONV_CH)
    row = lambda i: (i, 0)
    fixed = lambda i: (0, 0)
    return pl.pallas_call(
        functools.partial(_inproj_kernel, tiles_per_seq=tiles_per_seq),
        grid=(nt,),
        in_specs=[
            pl.BlockSpec((tm, D_MODEL), row),
            pl.BlockSpec((HALO, D_MODEL), lambda i: (jnp.maximum(i * halo_per_tile - 1, 0), 0)),
            pl.BlockSpec((HALO, D_MODEL), lambda i: (jnp.minimum((i + 1) * halo_per_tile, n_halo - 1), 0)),
            pl.BlockSpec((1, D_MODEL), fixed),
            pl.BlockSpec((1, D_MODEL), fixed),
            pl.BlockSpec((D_MODEL, IN_COLS_PAD), fixed),
            pl.BlockSpec((1, ATTN_W), fixed),
            pl.BlockSpec((1, KV_W), fixed),
            pl.BlockSpec((tm, LANES), lambda i: (i % tiles_per_seq, 0)),
            pl.BlockSpec((tm, LANES), lambda i: (i % tiles_per_seq, 0)),
            pl.BlockSpec((8, CONV_CH), fixed),
            pl.BlockSpec((1, CONV_CH), fixed),
        ],
        out_specs=[
            pl.BlockSpec((tm, ATTN_W), row),
            pl.BlockSpec((N_KV_HEADS, tm, HEAD_DIM), lambda i: (0, i, 0)),
            pl.BlockSpec((N_KV_HEADS, LANES, tm), lambda i: (0, 0, i)),
            pl.BlockSpec((tm, D_INNER), row),
            pl.BlockSpec((tm, CONV_CH), row),
            pl.BlockSpec((2, tm, SSD_HEADS), lambda i: (0, i, 0)),
        ],
        out_shape=[
            jax.ShapeDtypeStruct((tokens, ATTN_W), BF16),
            jax.ShapeDtypeStruct((N_KV_HEADS, tokens, HEAD_DIM), BF16),
            jax.ShapeDtypeStruct((N_KV_HEADS, LANES, tokens), BF16),
            jax.ShapeDtypeStruct((tokens, D_INNER), F32),
            jax.ShapeDtypeStruct((tokens, CONV_CH), F32),
            jax.ShapeDtypeStruct((2, tokens, SSD_HEADS), F32),
        ],
        scratch_shapes=[pltpu.VMEM((tm + 2 * HALO, CONV_CH), F32)],
        compiler_params=_cparams(("parallel",)),
        name="inproj",
    )(x2d, x2d, x2d, ln_g, ln_b, w_in_pad, qw, kw, cos, sin, cw, cb)


def _attn_kernel(q_ref, k_ref, vt_ref, o_ref):
    k = k_ref[...]
    vt = vt_ref[0:ATTN_V_ROWS, :]
    tq = ATTN_Q_SUB
    for t in range(q_ref.shape[0] // tq):
        q = q_ref[t * tq:(t + 1) * tq, :]
        qs = jnp.concatenate([q[:, h * HEAD_DIM:(h + 1) * HEAD_DIM] for h in range(Q_PER_KV)], axis=0)
        s_all = _dot_nt(qs, k)
        outs = []
        for h in range(Q_PER_KV):
            s = s_all[h * tq:(h + 1) * tq]
            m = jnp.max(s, axis=-1, keepdims=True)
            p = jnp.exp((s - m).astype(BF16))
            o = _dot_nt(vt, p).T
            outs.append(o[:, :HEAD_DIM] / o[:, HEAD_DIM:HEAD_DIM + 1])
        o_ref[t * tq:(t + 1) * tq, :] = jnp.concatenate(outs, axis=-1).astype(BF16)


def _attn_call(q, k, v, batch, seq):
    tokens = batch * seq
    tq = ATTN_Q_TILE
    nq = seq // tq
    width = Q_PER_KV * HEAD_DIM
    return pl.pallas_call(
        _attn_kernel,
        grid=(batch, N_KV_HEADS, nq),
        in_specs=[
            pl.BlockSpec((tq, width), lambda b, g, i: (b * nq + i, g)),
            pl.BlockSpec((None, seq, HEAD_DIM), lambda b, g, i: (g, b, 0)),
            pl.BlockSpec((None, LANES, seq), lambda b, g, i: (g, 0, b)),
        ],
        out_specs=pl.BlockSpec((tq, width), lambda b, g, i: (b * nq + i, g)),
        out_shape=jax.ShapeDtypeStruct((tokens, ATTN_W), BF16),
        compiler_params=_cparams(("parallel", "parallel", "parallel")),
        name="attn",
    )(q, k, v)


def _ssd_kernel(xc_ref, dt_ref, dtt_ref, dtb_ref, dtbt_ref, alog_ref, alogt_ref, dskip_ref,
                y_ref, state_ref):
    fwd = pl.program_id(1) == 0

    @pl.when(pl.program_id(2) == 0)
    def _():
        state_ref[...] = jnp.zeros_like(state_ref)

    for j in range(SSD_BLOCK):
        sub = jnp.where(fwd, j, SSD_BLOCK - 1 - j)
        _ssd_chunk(pl.multiple_of(sub * CHUNK, CHUNK), fwd, xc_ref, dt_ref, dtt_ref, dtb_ref, dtbt_ref,
                   alog_ref, alogt_ref, dskip_ref, y_ref, state_ref)


def _ssd_chunk(row0, fwd, xc_ref, dt_ref, dtt_ref, dtb_ref, dtbt_ref, alog_ref, alogt_ref, dskip_ref,
               y_ref, state_ref):
    rows = pl.ds(row0, CHUNK)
    xc = xc_ref[rows, :]
    xs = xc[:, :D_INNER]
    b_all = xc[:, D_INNER:D_INNER + SSD_GROUPS * D_STATE]
    c_all = xc[:, D_INNER + SSD_GROUPS * D_STATE:]

    dt = jax.nn.softplus(dt_ref[rows, :] + dtb_ref[...])
    a = dt * -jnp.exp(alog_ref[...])
    a_row = jax.nn.softplus(dtt_ref[:, rows] + dtbt_ref[...]) * -jnp.exp(alogt_ref[...])

    ri = lax.broadcasted_iota(I32, (CHUNK, CHUNK), 0)
    ci = lax.broadcasted_iota(I32, (CHUNK, CHUNK), 1)
    ahead = jnp.where(fwd, ri - ci, ci - ri)
    causal = ahead >= 0
    tri = jnp.where(causal, 1.0, 0.0).astype(BF16)
    tri_t = jnp.where(ahead <= 0, 1.0, 0.0).astype(BF16)
    cum = sum(_dot(tri, piece) for piece in _split3_bf16(a))
    cum_row = sum(_dot(piece, tri_t) for piece in _split3_bf16(a_row))
    total = jnp.sum(a, axis=0, keepdims=True)

    er = lax.broadcasted_iota(I32, (SSD_HEADS, D_INNER), 0)
    ec = lax.broadcasted_iota(I32, (SSD_HEADS, D_INNER), 1) // SSD_HEAD_DIM
    expand = jnp.where(er == ec, 1.0, 0.0).astype(BF16)
    stacked = jnp.concatenate([dt, jnp.exp(total - cum), jnp.exp(cum)], axis=0)
    stacked_e = sum(_dot(piece, expand) for piece in _split_bf16(stacked))
    dt_e = stacked_e[:CHUNK]
    dec_state_e = stacked_e[CHUNK:2 * CHUNK]
    dec_out_e = stacked_e[2 * CHUNK:]
    dec_chunk_e = sum(_dot(piece, expand) for piece in _split_bf16(jnp.exp(total)))

    xdt = xs * dt_e
    xdt_bf = xdt.astype(BF16)
    xdec_bf = (xdt * dec_state_e).astype(BF16)

    ys = []
    for g in range(SSD_GROUPS):
        bg = b_all[:, g * D_STATE:(g + 1) * D_STATE]
        cg = c_all[:, g * D_STATE:(g + 1) * D_STATE].astype(BF16)
        cb = _dot_nt(cg, bg.astype(BF16))
        lo = g * HEADS_PER_GROUP * SSD_HEAD_DIM
        hi = lo + HEADS_PER_GROUP * SSD_HEAD_DIM
        prev = state_ref[g]
        y_off = _dot(cg, prev.astype(BF16)) * dec_out_e[:, lo:hi]
        y_heads = []
        for r in range(HEADS_PER_GROUP):
            h = g * HEADS_PER_GROUP + r
            seg = cum[:, h:h + 1] - cum_row[h:h + 1, :]
            m = jnp.where(causal, cb * jnp.exp(jnp.where(causal, seg, 0.0)), 0.0)
            y_heads.append(_dot(m.astype(BF16), xdt_bf[:, h * SSD_HEAD_DIM:(h + 1) * SSD_HEAD_DIM]))
        ys.append(jnp.concatenate(y_heads, axis=-1) + y_off)
        st = _dot(bg.T.astype(BF16), xdec_bf[:, lo:hi])
        state_ref[g] = prev * dec_chunk_e[:, lo:hi] + st
    y = jnp.concatenate(ys, axis=-1)
    y_ref[rows, :] = y + jnp.where(fwd, 1.0, 0.0) * (dskip_ref[...] * xs)


def _ssd_call(xc, dt2, batch, seq, dt_bias, a_log, d_skip):
    tokens = batch * seq
    rows = SSD_BLOCK * CHUNK
    nc = seq // rows
    dtt = dt2.reshape(2, batch, seq, SSD_HEADS).transpose(1, 0, 3, 2).reshape(batch * 2 * SSD_HEADS, seq)
    dtb = dt_bias.reshape(2, 1, SSD_HEADS)
    dtbt = dt_bias.reshape(2, SSD_HEADS, 1)
    na = a_log.reshape(2, 1, SSD_HEADS)
    nat = a_log.reshape(2, SSD_HEADS, 1)
    dskip_e = jnp.repeat(d_skip, SSD_HEAD_DIM).reshape(1, D_INNER)

    def ceff(d, c):
        return jnp.where(d == 0, c, nc - 1 - c)

    def main_idx(b, d, c):
        return (b * nc + ceff(d, c), 0)

    sel_d = lambda b, d, c: (d, 0, 0)
    fixed = lambda b, d, c: (0, 0)
    return pl.pallas_call(
        _ssd_kernel,
        grid=(batch, 2, nc),
        in_specs=[
            pl.BlockSpec((rows, CONV_CH), main_idx),
            pl.BlockSpec((None, rows, SSD_HEADS), lambda b, d, c: (d, b * nc + ceff(d, c), 0)),
            pl.BlockSpec((SSD_HEADS, rows), lambda b, d, c: (b * 2 + d, ceff(d, c))),
            pl.BlockSpec((None, 1, SSD_HEADS), sel_d),
            pl.BlockSpec((None, SSD_HEADS, 1), sel_d),
            pl.BlockSpec((None, 1, SSD_HEADS), sel_d),
            pl.BlockSpec((None, SSD_HEADS, 1), sel_d),
            pl.BlockSpec((1, D_INNER), fixed),
        ],
        out_specs=pl.BlockSpec((None, rows, D_INNER), lambda b, d, c: (d, b * nc + ceff(d, c), 0)),
        out_shape=jax.ShapeDtypeStruct((2, tokens, D_INNER), F32),
        scratch_shapes=[
            pltpu.VMEM((SSD_GROUPS, D_STATE, HEADS_PER_GROUP * SSD_HEAD_DIM), F32),
        ],
        compiler_params=_cparams(("parallel", "arbitrary", "arbitrary")),
        name="ssd",
    )(xc, dt2, dtt, dtb, dtbt, na, nat, dskip_e)


def _mix_kernel(x_ref, ing_ref, inb_ref, attn_ref, yf_ref, yb_ref, z_ref, nw_ref, wo_ref,
                g_ref, b_ref, wrh_ref, wrl_ref, x1_ref, aff_ref):
    sub = x_ref.shape[0] // ROW_GROUPS
    for grp in range(ROW_GROUPS):
        rs = slice(grp * sub, (grp + 1) * sub)
        xln = _layer_norm_rows(x_ref[rs, :], ing_ref[...], inb_ref[...])
        z = z_ref[rs, :]
        y = (yf_ref[rs, :] + yb_ref[rs, :]) * (z * _sigmoid(z))
        gw = D_INNER // SSD_GROUPS
        parts = []
        for g in range(SSD_GROUPS):
            yg = y[:, g * gw:(g + 1) * gw]
            parts.append(yg * lax.rsqrt(jnp.mean(yg * yg, axis=-1, keepdims=True) + NORM_EPS))
        ssd = (jnp.concatenate(parts, axis=-1) * nw_ref[...]).astype(BF16)
        mix = _dot(jnp.concatenate([attn_ref[rs, :], ssd], axis=-1), wo_ref[...])
        x1 = _layer_norm_rows(DN_ALPHA * xln + mix, g_ref[...], b_ref[...])
        x1_ref[rs, :] = x1
        hi, lo = _split_bf16(x1)
        wrh = wrh_ref[...]
        logits = _dot(hi, wrh) + _dot(lo, wrh) + _dot(hi, wrl_ref[...])
        lane = lax.broadcasted_iota(I32, logits.shape, 1)
        logits = jnp.where(lane < N_EXPERTS, logits, -jnp.inf)
        e = jnp.exp(logits - jnp.max(logits, axis=-1, keepdims=True))
        aff = e / jnp.sum(e, axis=-1, keepdims=True)
        aff_ref[rs, :] = aff[:, :N_EXPERTS]


def _mix_call(x2d, ln_in_g, ln_in_b, attn, y2, z, norm_w, w_out, ln_g, ln_b, wr_hi, wr_lo):
    tokens = x2d.shape[0]
    tm = TOKEN_TILE
    row = lambda i: (i, 0)
    fixed = lambda i: (0, 0)
    vec = pl.BlockSpec((1, D_MODEL), fixed)
    return pl.pallas_call(
        _mix_kernel,
        grid=(tokens // tm,),
        in_specs=[
            pl.BlockSpec((tm, D_MODEL), row), vec, vec,
            pl.BlockSpec((tm, ATTN_W), row),
            pl.BlockSpec((None, tm, D_INNER), lambda i: (0, i, 0)),
            pl.BlockSpec((None, tm, D_INNER), lambda i: (1, i, 0)),
            pl.BlockSpec((tm, D_INNER), row),
            pl.BlockSpec((1, D_INNER), fixed),
            pl.BlockSpec((D_MODEL, D_MODEL), fixed),
            vec, vec,
            pl.BlockSpec((D_MODEL, LANES), fixed),
            pl.BlockSpec((D_MODEL, LANES), fixed),
        ],
        out_specs=[
            pl.BlockSpec((tm, D_MODEL), row),
            pl.BlockSpec((tm, N_EXPERTS), row),
        ],
        out_shape=[
            jax.ShapeDtypeStruct((tokens, D_MODEL), F32),
            jax.ShapeDtypeStruct((tokens, N_EXPERTS), F32),
        ],
        compiler_params=_cparams(("parallel",)),
        name="mix",
    )(x2d, ln_in_g, ln_in_b, attn, y2, y2, z, norm_w, w_out, ln_g, ln_b, wr_hi, wr_lo)


def _tile_cumsum(mask_bf, upper, strict_lower):
    wloc = _dot(mask_bf, upper)
    tot = jnp.broadcast_to(wloc[:, LANES - 1:LANES], wloc.shape).astype(BF16)
    offs = _dot(strict_lower, tot)
    return wloc, offs


def _select_kernel(aff_ref, idx_ref, lp_ref, win_ref, *, cap):
    for e in range(SEL_EXPERTS):
        _select_expert(aff_ref.at[e], idx_ref.at[e], lp_ref.at[e], win_ref.at[e], cap)


def _select_expert(aff_ref, idx_ref, lp_ref, win_ref, cap):
    nt = aff_ref.shape[0]
    aff = aff_ref[...]

    def enough(cand_bits):
        cnt = jnp.sum(jnp.where(aff >= pltpu.bitcast(cand_bits, F32), 1.0, 0.0), keepdims=True)
        return cnt >= cap

    t_bits = jnp.where(enough(jnp.full((1, 1), 1 << 30, I32)), 1 << 30, 0).astype(I32)
    for bit in range(29, 0, -2):
        hi, lo = 1 << bit, 1 << (bit - 1)
        c1, c2, c3 = t_bits | lo, t_bits | hi, t_bits | hi | lo
        t_bits = jnp.where(enough(c3), c3, jnp.where(enough(c2), c2, jnp.where(enough(c1), c1, t_bits)))
    t = pltpu.bitcast(t_bits, F32)
    gt = aff > t
    eq = aff == t
    need = cap - jnp.sum(jnp.where(gt, 1.0, 0.0), keepdims=True)

    ri = lax.broadcasted_iota(I32, (LANES, LANES), 0)
    ci = lax.broadcasted_iota(I32, (LANES, LANES), 1)
    upper = jnp.where(ri <= ci, 1.0, 0.0).astype(BF16)
    rt = lax.broadcasted_iota(I32, (nt, nt), 0)
    ct = lax.broadcasted_iota(I32, (nt, nt), 1)
    strict_lower = jnp.where(ct < rt, 1.0, 0.0).astype(BF16)

    eq_loc, eq_offs = _tile_cumsum(jnp.where(eq, 1.0, 0.0).astype(BF16), upper, strict_lower)
    sel = gt | (eq & (eq_loc + eq_offs <= need))
    sel_bf = jnp.where(sel, 1.0, 0.0).astype(BF16)
    wloc, offs = _tile_cumsum(sel_bf, upper, strict_lower)

    sel_tiles = COMB_TILE // LANES
    first_lower = jnp.where(ct < (rt // sel_tiles) * sel_tiles, 1.0, 0.0).astype(BF16)
    offs_first = _dot(first_lower, jnp.broadcast_to(wloc[:, LANES - 1:LANES], wloc.shape).astype(BF16))
    win = jnp.minimum(jnp.floor(offs_first * (1.0 / BF16_SUBLANES)) * BF16_SUBLANES, float(cap - COMB_SPAN))
    win_ref[...] = win
    lp_ref[...] = jnp.where(sel, wloc + offs - 1.0 - win, -1.0)

    tot = jnp.broadcast_to(wloc[:, LANES - 1:LANES], wloc.shape)
    tot_row = _dot_nt(jnp.ones((8, LANES), BF16), sel_bf)
    upper_t = jnp.where(rt <= ct, 1.0, 0.0).astype(BF16)
    incl_row = _dot(tot_row.astype(BF16), upper_t)[0:1, :]
    next_tile = jnp.where(ct == rt + 1, 1.0, 0.0).astype(BF16)
    wloc_step = _dot(next_tile, wloc.astype(BF16)) - wloc
    lane_id = lax.broadcasted_iota(I32, (nt, LANES), 1)
    side = jnp.where(lane_id == 0, 1.0, jnp.where(lane_id == 1, tot, 0.0))
    steps = jnp.concatenate([wloc_step, side], axis=-1).astype(BF16)
    first = jnp.concatenate([wloc[0:1, :], jnp.zeros((1, LANES), F32)], axis=-1)
    for s0 in range(0, cap, SEL_SLOT_CHUNK):
        n = min(SEL_SLOT_CHUNK, cap - s0)
        slot = (lax.broadcasted_iota(I32, (n, 1), 0) + s0).astype(F32)
        before = jnp.where(incl_row <= slot, 1.0, 0.0).astype(BF16)
        got = _dot(before, steps) + first
        tile = got[:, LANES:LANES + 1]
        base = got[:, LANES + 1:LANES + 2]
        lane = jnp.sum(jnp.where(got[:, :LANES] <= slot - base, 1.0, 0.0), axis=-1, keepdims=True)
        tok = jnp.broadcast_to(tile * LANES + lane, (n, LANES))
        idx_ref[:, s0:s0 + n] = tok.T[0:1, :].astype(I32)


def _select_call(aff):
    tokens = aff.shape[0]
    cap = CAPACITY_FACTOR * tokens // N_EXPERTS
    nt = tokens // LANES
    aff_t = aff.T.reshape(N_EXPERTS, nt, LANES)
    blk = pl.BlockSpec((SEL_EXPERTS, nt, LANES), lambda e: (e, 0, 0))
    idx, lp, win = pl.pallas_call(
        functools.partial(_select_kernel, cap=cap),
        grid=(N_EXPERTS // SEL_EXPERTS,),
        in_specs=[blk],
        out_specs=[pl.BlockSpec((SEL_EXPERTS, 1, cap), lambda e: (e, 0, 0)), blk, blk],
        out_shape=[
            jax.ShapeDtypeStruct((N_EXPERTS, 1, cap), I32),
            jax.ShapeDtypeStruct((N_EXPERTS, nt, LANES), F32),
            jax.ShapeDtypeStruct((N_EXPERTS, nt, LANES), F32),
        ],
        compiler_params=_cparams(("parallel",)),
        name="select",
    )(aff_t)
    lp_tok = lp.reshape(N_EXPERTS, tokens).T
    win_start = win[:, ::COMB_TILE // LANES, 0].astype(I32).reshape(-1)
    return idx.reshape(N_EXPERTS * cap), lp_tok, win_start, cap


def _ffn_kernel(idx_ref, x_hbm, wg_ref, wu_ref, wd_ref, ye_ref, land_a, land_b, xe_a, xe_b, sem_ref,
                *, cap):
    e = pl.program_id(0)
    c = pl.program_id(1)
    pairs = cap // (2 * FFN_ROWS)
    step = e * pairs + c
    n_steps = N_EXPERTS * pairs
    last = 2 * n_steps - 1
    land = (land_a, land_b)
    stage = (xe_a, xe_b)

    def row_copy(chunk, r, half):
        tok = idx_ref[chunk * FFN_ROWS + r]
        return pltpu.make_async_copy(x_hbm.at[pl.ds(tok, 1), :], land[half].at[pl.ds(r, 1), :],
                                     sem_ref.at[half])

    def start_chunk(chunk, half):
        for r in range(FFN_ROWS):
            row_copy(chunk, r, half).start()

    def wait_chunk(chunk, half):
        def body(r, carry):
            row_copy(chunk, r, half).wait()
            return carry
        lax.fori_loop(0, FFN_ROWS, body, 0, unroll=8)

    @pl.when(step == 0)
    def _():
        start_chunk(0, 0)
        start_chunk(1, 1)

    for half in range(2):
        wait_chunk(2 * step + half, half)
        stage[half][...] = land[half][...].astype(BF16)
        start_chunk(jnp.minimum(2 * step + 2 + half, last), half)
        xe = stage[half][...]
        gate = _dot(xe, wg_ref[...])
        up = _dot(xe, wu_ref[...])
        hid = (gate * _sigmoid(gate) * up).astype(BF16)
        ye_ref[half * FFN_ROWS:(half + 1) * FFN_ROWS, :] = _dot(hid, wd_ref[...]).astype(BF16)

    @pl.when(step == n_steps - 1)
    def _():
        wait_chunk(last, 0)
        wait_chunk(last, 1)


def _ffn_call(idx, x1, wg, wu, wd, cap):
    pairs = cap // (2 * FFN_ROWS)
    wspec = lambda d0, d1: pl.BlockSpec((None, d0, d1), lambda e, c, idx: (e, 0, 0))
    grid_spec = pltpu.PrefetchScalarGridSpec(
        num_scalar_prefetch=1,
        grid=(N_EXPERTS, pairs),
        in_specs=[
            pl.BlockSpec(memory_space=pl.ANY),
            wspec(D_MODEL, D_EXPERT), wspec(D_MODEL, D_EXPERT), wspec(D_EXPERT, D_MODEL),
        ],
        out_specs=pl.BlockSpec((None, 2 * FFN_ROWS, D_MODEL), lambda e, c, idx: (e, c, 0)),
        scratch_shapes=[
            pltpu.VMEM((FFN_ROWS, D_MODEL), F32),
            pltpu.VMEM((FFN_ROWS, D_MODEL), F32),
            pltpu.VMEM((FFN_ROWS, D_MODEL), BF16),
            pltpu.VMEM((FFN_ROWS, D_MODEL), BF16),
            pltpu.SemaphoreType.DMA((2,)),
        ],
    )
    return pl.pallas_call(
        functools.partial(_ffn_kernel, cap=cap),
        grid_spec=grid_spec,
        out_shape=jax.ShapeDtypeStruct((N_EXPERTS, cap, D_MODEL), BF16),
        compiler_params=_cparams(("arbitrary", "arbitrary")),
        name="ffn",
    )(idx, x1, wg, wu, wd)


def _combine_kernel(win_ref, x1_ref, p_ref, lp_ref, aff_ref, ye_hbm, wpg_ref, bpg_ref, wple_ref,
                    g_ref, b_ref, o_ref, buf_ref, xbuf_ref, acc_ref, sem_ref, xsem_ref, *, n_tiles):
    i = pl.program_id(0)
    slot = i % 2
    width = N_EXPERTS * COMB_WIN

    def win_copy(tile, e, rnd, dst, sem):
        start = pl.multiple_of(win_ref[e * n_tiles + tile] + rnd * COMB_WIN, BF16_SUBLANES)
        return pltpu.make_async_copy(ye_hbm.at[e, pl.ds(start, COMB_WIN), :],
                                     dst.at[pl.ds(e * COMB_WIN, COMB_WIN), :], sem)

    def start_tile(tile, buf_slot):
        for e in range(N_EXPERTS):
            win_copy(tile, e, 0, buf_ref.at[buf_slot], sem_ref.at[buf_slot]).start()

    @pl.when(i == 0)
    def _():
        start_tile(0, 0)

    @pl.when(i + 1 < n_tiles)
    def _():
        start_tile(i + 1, 1 - slot)

    x1 = x1_ref[...]
    ple = _sigmoid(_dot(x1.astype(BF16), wpg_ref[...]) + bpg_ref[...]) \
        * _dot(p_ref[...].astype(BF16), wple_ref[...])
    acc_ref[...] = DN_ALPHA * x1 + ple

    lp = lp_ref[...]
    rnd_of = jnp.floor(lp * (1.0 / COMB_WIN))
    row_of = lp - rnd_of * COMB_WIN
    er = lax.broadcasted_iota(I32, (N_EXPERTS, width), 0)
    ec = lax.broadcasted_iota(I32, (N_EXPERTS, width), 1) // COMB_WIN
    expand = jnp.where(er == ec, 1.0, 0.0).astype(BF16)
    lane_row = (lax.broadcasted_iota(I32, (COMB_TILE, width), 1) % COMB_WIN).astype(F32)
    hit = _dot(row_of.astype(BF16), expand) == lane_row
    rnd_e = _dot(rnd_of.astype(BF16), expand)
    g_hi, g_lo = _split_bf16(aff_ref[...])
    g_hi_e = jnp.where(hit, _dot(g_hi, expand), 0.0)
    g_lo_e = jnp.where(hit, _dot(g_lo, expand), 0.0)

    def add_round(rnd, rows):
        pick = rnd_e == rnd
        lhs = jnp.concatenate([jnp.where(pick, g_hi_e, 0.0).astype(BF16),
                               jnp.where(pick, g_lo_e, 0.0).astype(BF16)], axis=0)
        both = _dot(lhs, rows)
        acc_ref[...] += both[:COMB_TILE] + both[COMB_TILE:]

    for e in range(N_EXPERTS):
        win_copy(i, e, 0, buf_ref.at[slot], sem_ref.at[slot]).wait()
    add_round(0.0, buf_ref[slot])

    n_rounds = jnp.max(rnd_of).astype(I32) + 1
    for rnd in range(1, COMB_ROUNDS):
        @pl.when(rnd < n_rounds)
        def _():
            for e in range(N_EXPERTS):
                win_copy(i, e, rnd, xbuf_ref, xsem_ref.at[0]).start()
            for e in range(N_EXPERTS):
                win_copy(i, e, rnd, xbuf_ref, xsem_ref.at[0]).wait()
            add_round(float(rnd), xbuf_ref[...])

    o_ref[...] = _layer_norm_rows(acc_ref[...], g_ref[...], b_ref[...])


def _combine_call(win_start, x1, p2d, lp_tok, aff, ye, wpg, bpg, wple, ln_g, ln_b):
    tokens = x1.shape[0]
    n_tiles = tokens // COMB_TILE
    row = lambda i, w: (i, 0)
    fixed = lambda i, w: (0, 0)
    vec = pl.BlockSpec((1, D_MODEL), fixed)
    grid_spec = pltpu.PrefetchScalarGridSpec(
        num_scalar_prefetch=1,
        grid=(n_tiles,),
        in_specs=[
            pl.BlockSpec((COMB_TILE, D_MODEL), row),
            pl.BlockSpec((COMB_TILE, PLE_DIM), row),
            pl.BlockSpec((COMB_TILE, N_EXPERTS), row),
            pl.BlockSpec((COMB_TILE, N_EXPERTS), row),
            pl.BlockSpec(memory_space=pl.ANY),
            pl.BlockSpec((D_MODEL, D_MODEL), fixed),
            vec,
            pl.BlockSpec((PLE_DIM, D_MODEL), fixed),
            vec, vec,
        ],
        out_specs=pl.BlockSpec((COMB_TILE, D_MODEL), row),
        scratch_shapes=[
            pltpu.VMEM((2, N_EXPERTS * COMB_WIN, D_MODEL), BF16),
            pltpu.VMEM((N_EXPERTS * COMB_WIN, D_MODEL), BF16),
            pltpu.VMEM((COMB_TILE, D_MODEL), F32),
            pltpu.SemaphoreType.DMA((2,)),
            pltpu.SemaphoreType.DMA((1,)),
        ],
    )
    return pl.pallas_call(
        functools.partial(_combine_kernel, n_tiles=n_tiles),
        grid_spec=grid_spec,
        out_shape=jax.ShapeDtypeStruct((tokens, D_MODEL), F32),
        compiler_params=_cparams(("arbitrary",)),
        name="combine",
    )(win_start, x1, p2d, lp_tok, aff, ye, wpg, bpg, wple, ln_g, ln_b)


def _prep_weights(ln_in_g, ln_in_b, w_in, q_norm_w, k_norm_w, conv_w, conv_b, dt_bias, a_log,
                  d_skip, ssd_norm_w, w_out, ln1_g, ln1_b, w_router, w_e_gate, w_e_up, w_e_down,
                  w_ple_gate, b_ple_gate, w_ple, ln2_g, ln2_b):
    l = 0
    vec = lambda a: a.reshape(1, -1)
    w_in_pad = jnp.zeros((D_MODEL, IN_COLS_PAD), F32).at[:, :IN_COLS].set(w_in[l]).astype(BF16)
    wr = jnp.zeros((D_MODEL, LANES), F32).at[:, :N_EXPERTS].set(w_router[l])
    wr_hi, wr_lo = _split_bf16(wr)
    return dict(
        ln_in_g=vec(ln_in_g), ln_in_b=vec(ln_in_b), w_in=w_in_pad,
        qw=vec(jnp.tile(q_norm_w[l], N_Q_HEADS)), kw=vec(jnp.tile(k_norm_w[l], N_KV_HEADS)),
        conv_w=conv_w[l], conv_b=conv_b[l], dt_bias=dt_bias[l], a_log=a_log[l], d_skip=d_skip[l],
        norm_w=vec(ssd_norm_w[l]), w_out=w_out[l].astype(BF16), ln1_g=vec(ln1_g[l]), ln1_b=vec(ln1_b[l]),
        wr_hi=wr_hi, wr_lo=wr_lo,
        wg=w_e_gate[l].astype(BF16), wu=w_e_up[l].astype(BF16), wd=w_e_down[l].astype(BF16),
        wpg=w_ple_gate[l].astype(BF16), bpg=vec(b_ple_gate[l]), wple=w_ple[l].astype(BF16),
        ln2_g=vec(ln2_g[l]), ln2_b=vec(ln2_b[l]),
    )


def _trunk(x, p, w):
    batch, seq, _ = x.shape
    tokens = batch * seq
    x2d = x.reshape(tokens, D_MODEL)
    q, k, v, z, xc, dt2 = _inproj_call(x2d, seq, w["ln_in_g"], w["ln_in_b"], w["w_in"], w["qw"], w["kw"],
                                       w["conv_w"], w["conv_b"])
    attn = _attn_call(q, k, v, batch, seq)
    y2 = _ssd_call(xc, dt2, batch, seq, w["dt_bias"], w["a_log"], w["d_skip"])
    x1, aff = _mix_call(x2d, w["ln_in_g"], w["ln_in_b"], attn, y2, z, w["norm_w"], w["w_out"],
                        w["ln1_g"], w["ln1_b"], w["wr_hi"], w["wr_lo"])
    idx, lp_tok, win_start, cap = _select_call(aff)
    ye = _ffn_call(idx, x1, w["wg"], w["wu"], w["wd"], cap)
    out = _combine_call(win_start, x1, p[0].reshape(tokens, PLE_DIM), lp_tok, aff, ye,
                        w["wpg"], w["bpg"], w["wple"], w["ln2_g"], w["ln2_b"])
    return out.reshape(batch, seq, D_MODEL)


def kernel(x_prompt, x_sample, p_prompt, p_sample, ln_in_g, ln_in_b, w_in, q_norm_w, k_norm_w, conv_w, conv_b, dt_bias, a_log, d_skip, ssd_norm_w, w_out, ln1_g, ln1_b, w_router, w_e_gate, w_e_up, w_e_down, w_ple_gate, b_ple_gate, w_ple, ln2_g, ln2_b):
    w = _prep_weights(ln_in_g, ln_in_b, w_in, q_norm_w, k_norm_w, conv_w, conv_b, dt_bias, a_log,
                      d_skip, ssd_norm_w, w_out, ln1_g, ln1_b, w_router, w_e_gate, w_e_up, w_e_down,
                      w_ple_gate, b_ple_gate, w_ple, ln2_g, ln2_b)
    return (_trunk(x_prompt, p_prompt, w), _trunk(x_sample, p_sample, w))
```

```python
import functools
import math

import jax
import jax.numpy as jnp
from jax import lax
from jax.experimental import pallas as pl
from jax.experimental.pallas import tpu as pltpu

F32 = jnp.float32
BF16 = jnp.bfloat16
I32 = jnp.int32

D_MODEL = 1024
DEPTH = 1
GRID_W = 64
N_Q_HEADS = 8
N_KV_HEADS = 2
HEAD_DIM = 64
Q_PER_KV = N_Q_HEADS // N_KV_HEADS
ATTN_W = N_Q_HEADS * HEAD_DIM
KV_W = N_KV_HEADS * HEAD_DIM
ROPE_THETA = 10000.0
SSD_HEADS = 8
SSD_HEAD_DIM = 64
D_INNER = SSD_HEADS * SSD_HEAD_DIM
SSD_GROUPS = 2
HEADS_PER_GROUP = SSD_HEADS // SSD_GROUPS
D_STATE = 128
D_CONV = 5
CHUNK = 128
CONV_CH = D_INNER + 2 * SSD_GROUPS * D_STATE
K0 = ATTN_W
V0 = K0 + KV_W
Z0 = V0 + KV_W
XBC0 = Z0 + D_INNER
DT0 = XBC0 + CONV_CH
IN_COLS = DT0 + 2 * SSD_HEADS
N_EXPERTS = 16
CAPACITY_FACTOR = 2
D_EXPERT = 1024
PLE_DIM = 256
DN_ALPHA = (2.0 * DEPTH) ** 0.25
NORM_EPS = 1e-6
LN_EPS = 1e-5

LANES = 128
BF16_SUBLANES = 16
IN_COLS_PAD = DT0 + LANES
TOKEN_TILE = 1024
ROW_GROUPS = 4
HALO = 8
SSD_BLOCK = 16
ATTN_Q_TILE = 1024
ATTN_Q_SUB = 256
ATTN_V_ROWS = HEAD_DIM + BF16_SUBLANES
FFN_ROWS = 256
COMB_TILE = 256
COMB_WIN = 64
COMB_ROUNDS = -(-(COMB_TILE + BF16_SUBLANES - 1) // COMB_WIN)
COMB_SPAN = COMB_ROUNDS * COMB_WIN
SEL_SLOT_CHUNK = 1024
SEL_EXPERTS = 2
VMEM_LIMIT = 56 * 1024 * 1024


def _cparams(sem):
    return pltpu.CompilerParams(dimension_semantics=sem, vmem_limit_bytes=VMEM_LIMIT)


def _layer_norm_rows(x, g, b):
    xc = x - jnp.mean(x, axis=-1, keepdims=True)
    return xc * lax.rsqrt(jnp.mean(xc * xc, axis=-1, keepdims=True) + LN_EPS) * g + b


def _sigmoid(x):
    return 1.0 / (1.0 + jnp.exp(-x))


def _dot(a, b):
    return jnp.dot(a, b, preferred_element_type=F32)


def _dot_nt(a, b):
    return lax.dot_general(a, b, (((1,), (1,)), ((), ())), preferred_element_type=F32)


def _split_bf16(x):
    hi = x.astype(BF16)
    lo = (x - hi.astype(F32)).astype(BF16)
    return hi, lo


def _split3_bf16(x):
    hi = x.astype(BF16)
    rest = x - hi.astype(F32)
    mid = rest.astype(BF16)
    lo = (rest - mid.astype(F32)).astype(BF16)
    return hi, mid, lo


def _rope_swap(x):
    outs = []
    for c in range(x.shape[-1] // LANES):
        xc = x[:, c * LANES:(c + 1) * LANES]
        lane = lax.broadcasted_iota(I32, xc.shape, 1)
        up = pltpu.roll(xc, LANES - 16, axis=1)
        dn = pltpu.roll(xc, 16, axis=1)
        outs.append(jnp.where((lane % 32) < 16, up, dn))
    return outs[0] if len(outs) == 1 else jnp.concatenate(outs, axis=-1)


def _head_rms(x, w):
    width = x.shape[-1]
    r = lax.broadcasted_iota(I32, (width, width), 0) // HEAD_DIM
    c = lax.broadcasted_iota(I32, (width, width), 1) // HEAD_DIM
    blockdiag = jnp.where(r == c, 1.0, 0.0).astype(BF16)
    hi, lo = _split_bf16(x * x)
    ssq = _dot(hi, blockdiag) + _dot(lo, blockdiag)
    return x * lax.rsqrt(ssq * (1.0 / HEAD_DIM) + NORM_EPS) * w


def _inproj_kernel(x_ref, xprev_ref, xnext_ref, g_ref, b_ref, w_ref, qw_ref, kw_ref, cos_ref, sin_ref,
                   cw_ref, cb_ref, q_ref, k_ref, v_ref, z_ref, xc_ref, dt_ref, pad_ref, *, tiles_per_seq):
    tm = x_ref.shape[0]
    sub = tm // ROW_GROUPS

    pos = pl.program_id(0) % tiles_per_seq
    has_prev = (pos > 0).astype(F32)
    has_next = (pos < tiles_per_seq - 1).astype(F32)
    edge = jnp.concatenate([xprev_ref[...], xnext_ref[...]], axis=0)
    edge = _dot(_layer_norm_rows(edge, g_ref[...], b_ref[...]).astype(BF16), w_ref[:, XBC0:DT0])
    pad_ref[0:HALO, :] = edge[:HALO] * has_prev
    pad_ref[HALO + tm:, :] = edge[HALO:] * has_next

    for grp in range(ROW_GROUPS):
        rs = slice(grp * sub, (grp + 1) * sub)
        xln = _layer_norm_rows(x_ref[rs, :], g_ref[...], b_ref[...])
        proj = _dot(xln.astype(BF16), w_ref[...])
        cos = cos_ref[rs, :]
        sin = sin_ref[rs, :]
        q = _head_rms(proj[:, :K0], qw_ref[...])
        cos_q = jnp.concatenate([cos] * (ATTN_W // LANES), axis=-1)
        sin_q = jnp.concatenate([sin] * (ATTN_W // LANES), axis=-1)
        q = (q * cos_q + _rope_swap(q) * sin_q) * (HEAD_DIM ** -0.5)
        q_ref[rs, :] = q.astype(BF16)
        k = _head_rms(proj[:, K0:V0], kw_ref[...])
        k = (k * cos + _rope_swap(k) * sin).astype(BF16)
        v = proj[:, V0:Z0]
        ones_col = jnp.where(lax.broadcasted_iota(I32, (sub, LANES - HEAD_DIM), 1) == 0, 1.0, 0.0)
        for g in range(N_KV_HEADS):
            k_ref[g, rs, :] = k[:, g * HEAD_DIM:(g + 1) * HEAD_DIM]
            v_ext = jnp.concatenate([v[:, g * HEAD_DIM:(g + 1) * HEAD_DIM], ones_col], axis=-1)
            v_ref[g, :, rs] = v_ext.T.astype(BF16)
        z_ref[rs, :] = proj[:, Z0:XBC0]
        pad_ref[HALO + grp * sub:HALO + (grp + 1) * sub, :] = proj[:, XBC0:DT0]
        dt = proj[:, DT0:DT0 + LANES]
        for d in range(2):
            dt_ref[d, rs, :] = dt[:, d * SSD_HEADS:(d + 1) * SSD_HEADS]

    cw = cw_ref[...]
    for grp in range(ROW_GROUPS):
        window = pad_ref[grp * sub:grp * sub + sub + 2 * HALO, :]
        acc = jnp.broadcast_to(cb_ref[...], (sub, CONV_CH))
        for tap in range(D_CONV):
            shift = (D_CONV // 2 - tap) % (sub + 2 * HALO)
            shifted = window if shift == 0 else pltpu.roll(window, shift, axis=0)
            acc = acc + shifted[HALO:HALO + sub, :] * cw[tap:tap + 1, :]
        xc_ref[grp * sub:(grp + 1) * sub, :] = acc * _sigmoid(acc)


def _rope_tables(seq):
    rows = seq // GRID_W
    row_id = jnp.broadcast_to(jnp.arange(rows)[:, None], (rows, GRID_W)).reshape(seq)
    col_id = jnp.broadcast_to(jnp.arange(GRID_W)[None, :], (rows, GRID_W)).reshape(seq)
    pos = jnp.stack([row_id, col_id], axis=-1).astype(F32)
    n_freq = HEAD_DIM // 4
    inv_freq = ROPE_THETA ** (-jnp.arange(n_freq, dtype=F32) / n_freq)
    ang = pos[:, :, None] * inv_freq
    cos = jnp.broadcast_to(jnp.cos(ang)[:, :, None, :], (seq, 2, 2, n_freq)).reshape(seq, HEAD_DIM)
    sin = jnp.broadcast_to(jnp.sin(ang)[:, :, None, :], (seq, 2, 2, n_freq))
    sin = (sin * jnp.array([-1.0, 1.0], F32)[None, None, :, None]).reshape(seq, HEAD_DIM)
    reps = LANES // HEAD_DIM
    return jnp.tile(cos, (1, reps)), jnp.tile(sin, (1, reps))


def _inproj_call(x2d, seq, ln_g, ln_b, w_in_pad, qw, kw, conv_w, conv_b):
    tokens = x2d.shape[0]
    tm = TOKEN_TILE
    nt = tokens // tm
    tiles_per_seq = seq // tm
    halo_per_tile = tm // HALO
    n_halo = tokens // HALO
    cos, sin = _rope_tables(seq)
    cw = jnp.zeros((8, CONV_CH), F32).at[:D_CONV].set(conv_w)
    cb = conv_b.reshape(1, CONV_CH)
    row = lambda i: (i, 0)
    fixed = lambda i: (0, 0)
    return pl.pallas_call(
        functools.partial(_inproj_kernel, tiles_per_seq=tiles_per_seq),
        grid=(nt,),
        in_specs=[
            pl.BlockSpec((tm, D_MODEL), row),
            pl.BlockSpec((HALO, D_MODEL), lambda i: (jnp.maximum(i * halo_per_tile - 1, 0), 0)),
            pl.BlockSpec((HALO, D_MODEL), lambda i: (jnp.minimum((i + 1) * halo_per_tile, n_halo - 1), 0)),
            pl.BlockSpec((1, D_MODEL), fixed),
            pl.BlockSpec((1, D_MODEL), fixed),
            pl.BlockSpec((D_MODEL, IN_COLS_PAD), fixed),
            pl.BlockSpec((1, ATTN_W), fixed),
            pl.BlockSpec((1, KV_W), fixed),
            pl.BlockSpec((tm, LANES), lambda i: (i % tiles_per_seq, 0)),
            pl.BlockSpec((tm, LANES), lambda i: (i % tiles_per_seq, 0)),
            pl.BlockSpec((8, CONV_CH), fixed),
            pl.BlockSpec((1, CONV_CH), fixed),
        ],
        out_specs=[
            pl.BlockSpec((tm, ATTN_W), row),
            pl.BlockSpec((N_KV_HEADS, tm, HEAD_DIM), lambda i: (0, i, 0)),
            pl.BlockSpec((N_KV_HEADS, LANES, tm), lambda i: (0, 0, i)),
            pl.BlockSpec((tm, D_INNER), row),
            pl.BlockSpec((tm, CONV_CH), row),
            pl.BlockSpec((2, tm, SSD_HEADS), lambda i: (0, i, 0)),
        ],
        out_shape=[
            jax.ShapeDtypeStruct((tokens, ATTN_W), BF16),
            jax.ShapeDtypeStruct((N_KV_HEADS, tokens, HEAD_DIM), BF16),
            jax.ShapeDtypeStruct((N_KV_HEADS, LANES, tokens), BF16),
            jax.ShapeDtypeStruct((tokens, D_INNER), F32),
            jax.ShapeDtypeStruct((tokens, CONV_CH), F32),
            jax.ShapeDtypeStruct((2, tokens, SSD_HEADS), F32),
        ],
        scratch_shapes=[pltpu.VMEM((tm + 2 * HALO, CONV_CH), F32)],
        compiler_params=_cparams(("parallel",)),
        name="inproj",
    )(x2d, x2d, x2d, ln_g, ln_b, w_in_pad, qw, kw, cos, sin, cw, cb)


def _attn_kernel(q_ref, k_ref, vt_ref, o_ref):
    k = k_ref[...]
    vt = vt_ref[0:ATTN_V_ROWS, :]
    tq = ATTN_Q_SUB
    for t in range(q_ref.shape[0] // tq):
        q = q_ref[t * tq:(t + 1) * tq, :]
        qs = jnp.concatenate([q[:, h * HEAD_DIM:(h + 1) * HEAD_DIM] for h in range(Q_PER_KV)], axis=0)
        s_all = _dot_nt(qs, k)
        outs = []
        for h in range(Q_PER_KV):
            s = s_all[h * tq:(h + 1) * tq]
            m = jnp.max(s, axis=-1, keepdims=True)
            p = jnp.exp((s - m).astype(BF16))
            o = _dot_nt(vt, p).T
            outs.append(o[:, :HEAD_DIM] / o[:, HEAD_DIM:HEAD_DIM + 1])
        o_ref[t * tq:(t + 1) * tq, :] = jnp.concatenate(outs, axis=-1).astype(BF16)


def _attn_call(q, k, v, batch, seq):
    tokens = batch * seq
    tq = ATTN_Q_TILE
    nq = seq // tq
    width = Q_PER_KV * HEAD_DIM
    return pl.pallas_call(
        _attn_kernel,
        grid=(batch, N_KV_HEADS, nq),
        in_specs=[
            pl.BlockSpec((tq, width), lambda b, g, i: (b * nq + i, g)),
            pl.BlockSpec((None, seq, HEAD_DIM), lambda b, g, i: (g, b, 0)),
            pl.BlockSpec((None, LANES, seq), lambda b, g, i: (g, 0, b)),
        ],
        out_specs=pl.BlockSpec((tq, width), lambda b, g, i: (b * nq + i, g)),
        out_shape=jax.ShapeDtypeStruct((tokens, ATTN_W), BF16),
        compiler_params=_cparams(("parallel", "parallel", "parallel")),
        name="attn",
    )(q, k, v)


def _ssd_kernel(xc_ref, dt_ref, dtt_ref, dtb_ref, dtbt_ref, alog_ref, alogt_ref, dskip_ref,
                y_ref, state_ref):
    fwd = pl.program_id(1) == 0

    @pl.when(pl.program_id(2) == 0)
    def _():
        state_ref[...] = jnp.zeros_like(state_ref)

    for j in range(SSD_BLOCK):
        sub = jnp.where(fwd, j, SSD_BLOCK - 1 - j)
        _ssd_chunk(pl.multiple_of(sub * CHUNK, CHUNK), fwd, xc_ref, dt_ref, dtt_ref, dtb_ref, dtbt_ref,
                   alog_ref, alogt_ref, dskip_ref, y_ref, state_ref)


def _ssd_chunk(row0, fwd, xc_ref, dt_ref, dtt_ref, dtb_ref, dtbt_ref, alog_ref, alogt_ref, dskip_ref,
               y_ref, state_ref):
    rows = pl.ds(row0, CHUNK)
    xc = xc_ref[rows, :]
    xs = xc[:, :D_INNER]
    b_all = xc[:, D_INNER:D_INNER + SSD_GROUPS * D_STATE]
    c_all = xc[:, D_INNER + SSD_GROUPS * D_STATE:]

    dt = jax.nn.softplus(dt_ref[rows, :] + dtb_ref[...])
    a = dt * -jnp.exp(alog_ref[...])
    a_row = jax.nn.softplus(dtt_ref[:, rows] + dtbt_ref[...]) * -jnp.exp(alogt_ref[...])

    ri = lax.broadcasted_iota(I32, (CHUNK, CHUNK), 0)
    ci = lax.broadcasted_iota(I32, (CHUNK, CHUNK), 1)
    ahead = jnp.where(fwd, ri - ci, ci - ri)
    causal = ahead >= 0
    tri = jnp.where(causal, 1.0, 0.0).astype(BF16)
    tri_t = jnp.where(ahead <= 0, 1.0, 0.0).astype(BF16)
    cum = sum(_dot(tri, piece) for piece in _split3_bf16(a))
    cum_row = sum(_dot(piece, tri_t) for piece in _split3_bf16(a_row))
    total = jnp.sum(a, axis=0, keepdims=True)

    er = lax.broadcasted_iota(I32, (SSD_HEADS, D_INNER), 0)
    ec = lax.broadcasted_iota(I32, (SSD_HEADS, D_INNER), 1) // SSD_HEAD_DIM
    expand = jnp.where(er == ec, 1.0, 0.0).astype(BF16)
    stacked = jnp.concatenate([dt, jnp.exp(total - cum), jnp.exp(cum)], axis=0)
    stacked_e = sum(_dot(piece, expand) for piece in _split_bf16(stacked))
    dt_e = stacked_e[:CHUNK]
    dec_state_e = stacked_e[CHUNK:2 * CHUNK]
    dec_out_e = stacked_e[2 * CHUNK:]
    dec_chunk_e = sum(_dot(piece, expand) for piece in _split_bf16(jnp.exp(total)))

    xdt = xs * dt_e
    xdt_bf = xdt.astype(BF16)
    xdec_bf = (xdt * dec_state_e).astype(BF16)

    ys = []
    for g in range(SSD_GROUPS):
        bg = b_all[:, g * D_STATE:(g + 1) * D_STATE]
        cg = c_all[:, g * D_STATE:(g + 1) * D_STATE].astype(BF16)
        cb = _dot_nt(cg, bg.astype(BF16))
        lo = g * HEADS_PER_GROUP * SSD_HEAD_DIM
        hi = lo + HEADS_PER_GROUP * SSD_HEAD_DIM
        prev = state_ref[g]
        y_off = _dot(cg, prev.astype(BF16)) * dec_out_e[:, lo:hi]
        y_heads = []
        for r in range(HEADS_PER_GROUP):
            h = g * HEADS_PER_GROUP + r
            seg = cum[:, h:h + 1] - cum_row[h:h + 1, :]
            m = jnp.where(causal, cb * jnp.exp(jnp.where(causal, seg, 0.0)), 0.0)
            y_heads.append(_dot(m.astype(BF16), xdt_bf[:, h * SSD_HEAD_DIM:(h + 1) * SSD_HEAD_DIM]))
        ys.append(jnp.concatenate(y_heads, axis=-1) + y_off)
        st = _dot(bg.T.astype(BF16), xdec_bf[:, lo:hi])
        state_ref[g] = prev * dec_chunk_e[:, lo:hi] + st
    y = jnp.concatenate(ys, axis=-1)
    y_ref[rows, :] = y + jnp.where(fwd, 1.0, 0.0) * (dskip_ref[...] * xs)


def _ssd_call(xc, dt2, batch, seq, dt_bias, a_log, d_skip):
    tokens = batch * seq
    rows = SSD_BLOCK * CHUNK
    nc = seq // rows
    dtt = dt2.reshape(2, batch, seq, SSD_HEADS).transpose(1, 0, 3, 2).reshape(batch * 2 * SSD_HEADS, seq)
    dtb = dt_bias.reshape(2, 1, SSD_HEADS)
    dtbt = dt_bias.reshape(2, SSD_HEADS, 1)
    na = a_log.reshape(2, 1, SSD_HEADS)
    nat = a_log.reshape(2, SSD_HEADS, 1)
    dskip_e = jnp.repeat(d_skip, SSD_HEAD_DIM).reshape(1, D_INNER)

    def ceff(d, c):
        return jnp.where(d == 0, c, nc - 1 - c)

    def main_idx(b, d, c):
        return (b * nc + ceff(d, c), 0)

    sel_d = lambda b, d, c: (d, 0, 0)
    fixed = lambda b, d, c: (0, 0)
    return pl.pallas_call(
        _ssd_kernel,
        grid=(batch, 2, nc),
        in_specs=[
            pl.BlockSpec((rows, CONV_CH), main_idx),
            pl.BlockSpec((None, rows, SSD_HEADS), lambda b, d, c: (d, b * nc + ceff(d, c), 0)),
            pl.BlockSpec((SSD_HEADS, rows), lambda b, d, c: (b * 2 + d, ceff(d, c))),
            pl.BlockSpec((None, 1, SSD_HEADS), sel_d),
            pl.BlockSpec((None, SSD_HEADS, 1), sel_d),
            pl.BlockSpec((None, 1, SSD_HEADS), sel_d),
            pl.BlockSpec((None, SSD_HEADS, 1), sel_d),
            pl.BlockSpec((1, D_INNER), fixed),
        ],
        out_specs=pl.BlockSpec((None, rows, D_INNER), lambda b, d, c: (d, b * nc + ceff(d, c), 0)),
        out_shape=jax.ShapeDtypeStruct((2, tokens, D_INNER), F32),
        scratch_shapes=[
            pltpu.VMEM((SSD_GROUPS, D_STATE, HEADS_PER_GROUP * SSD_HEAD_DIM), F32),
        ],
        compiler_params=_cparams(("parallel", "arbitrary", "arbitrary")),
        name="ssd",
    )(xc, dt2, dtt, dtb, dtbt, na, nat, dskip_e)


def _mix_kernel(x_ref, ing_ref, inb_ref, attn_ref, yf_ref, yb_ref, z_ref, nw_ref, wo_ref,
                g_ref, b_ref, wrh_ref, wrl_ref, x1_ref, aff_ref):
    sub = x_ref.shape[0] // ROW_GROUPS
    for grp in range(ROW_GROUPS):
        rs = slice(grp * sub, (grp + 1) * sub)
        xln = _layer_norm_rows(x_ref[rs, :], ing_ref[...], inb_ref[...])
        z = z_ref[rs, :]
        y = (yf_ref[rs, :] + yb_ref[rs, :]) * (z * _sigmoid(z))
        gw = D_INNER // SSD_GROUPS
        parts = []
        for g in range(SSD_GROUPS):
            yg = y[:, g * gw:(g + 1) * gw]
            parts.append(yg * lax.rsqrt(jnp.mean(yg * yg, axis=-1, keepdims=True) + NORM_EPS))
        ssd = (jnp.concatenate(parts, axis=-1) * nw_ref[...]).astype(BF16)
        mix = _dot(jnp.concatenate([attn_ref[rs, :], ssd], axis=-1), wo_ref[...])
        x1 = _layer_norm_rows(DN_ALPHA * xln + mix, g_ref[...], b_ref[...])
        x1_ref[rs, :] = x1
        hi, lo = _split_bf16(x1)
        wrh = wrh_ref[...]
        logits = _dot(hi, wrh) + _dot(lo, wrh) + _dot(hi, wrl_ref[...])
        lane = lax.broadcasted_iota(I32, logits.shape, 1)
        logits = jnp.where(lane < N_EXPERTS, logits, -jnp.inf)
        e = jnp.exp(logits - jnp.max(logits, axis=-1, keepdims=True))
        aff = e / jnp.sum(e, axis=-1, keepdims=True)
        aff_ref[rs, :] = aff[:, :N_EXPERTS]


def _mix_call(x2d, ln_in_g, ln_in_b, attn, y2, z, norm_w, w_out, ln_g, ln_b, wr_hi, wr_lo):
    tokens = x2d.shape[0]
    tm = TOKEN_TILE
    row = lambda i: (i, 0)
    fixed = lambda i: (0, 0)
    vec = pl.BlockSpec((1, D_MODEL), fixed)
    return pl.pallas_call(
        _mix_kernel,
        grid=(tokens // tm,),
        in_specs=[
            pl.BlockSpec((tm, D_MODEL), row), vec, vec,
            pl.BlockSpec((tm, ATTN_W), row),
            pl.BlockSpec((None, tm, D_INNER), lambda i: (0, i, 0)),
            pl.BlockSpec((None, tm, D_INNER), lambda i: (1, i, 0)),
            pl.BlockSpec((tm, D_INNER), row),
            pl.BlockSpec((1, D_INNER), fixed),
            pl.BlockSpec((D_MODEL, D_MODEL), fixed),
            vec, vec,
            pl.BlockSpec((D_MODEL, LANES), fixed),
            pl.BlockSpec((D_MODEL, LANES), fixed),
        ],
        out_specs=[
            pl.BlockSpec((tm, D_MODEL), row),
            pl.BlockSpec((tm, N_EXPERTS), row),
        ],
        out_shape=[
            jax.ShapeDtypeStruct((tokens, D_MODEL), F32),
            jax.ShapeDtypeStruct((tokens, N_EXPERTS), F32),
        ],
        compiler_params=_cparams(("parallel",)),
        name="mix",
    )(x2d, ln_in_g, ln_in_b, attn, y2, y2, z, norm_w, w_out, ln_g, ln_b, wr_hi, wr_lo)


def _tile_cumsum(mask_bf, upper, strict_lower):
    wloc = _dot(mask_bf, upper)
    tot = jnp.broadcast_to(wloc[:, LANES - 1:LANES], wloc.shape).astype(BF16)
    offs = _dot(strict_lower, tot)
    return wloc, offs


def _select_kernel(aff_ref, idx_ref, lp_ref, win_ref, *, cap):
    for e in range(SEL_EXPERTS):
        _select_expert(aff_ref.at[e], idx_ref.at[e], lp_ref.at[e], win_ref.at[e], cap)


def _select_expert(aff_ref, idx_ref, lp_ref, win_ref, cap):
    nt = aff_ref.shape[0]
    aff = aff_ref[...]

    def enough(cand_bits):
        cnt = jnp.sum(jnp.where(aff >= pltpu.bitcast(cand_bits, F32), 1.0, 0.0), keepdims=True)
        return cnt >= cap

    t_bits = jnp.where(enough(jnp.full((1, 1), 1 << 30, I32)), 1 << 30, 0).astype(I32)
    for bit in range(29, 0, -2):
        hi, lo = 1 << bit, 1 << (bit - 1)
        c1, c2, c3 = t_bits | lo, t_bits | hi, t_bits | hi | lo
        t_bits = jnp.where(enough(c3), c3, jnp.where(enough(c2), c2, jnp.where(enough(c1), c1, t_bits)))
    t = pltpu.bitcast(t_bits, F32)
    gt = aff > t
    eq = aff == t
    need = cap - jnp.sum(jnp.where(gt, 1.0, 0.0), keepdims=True)

    ri = lax.broadcasted_iota(I32, (LANES, LANES), 0)
    ci = lax.broadcasted_iota(I32, (LANES, LANES), 1)
    upper = jnp.where(ri <= ci, 1.0, 0.0).astype(BF16)
    rt = lax.broadcasted_iota(I32, (nt, nt), 0)
    ct = lax.broadcasted_iota(I32, (nt, nt), 1)
    strict_lower = jnp.where(ct < rt, 1.0, 0.0).astype(BF16)

    eq_loc, eq_offs = _tile_cumsum(jnp.where(eq, 1.0, 0.0).astype(BF16), upper, strict_lower)
    sel = gt | (eq & (eq_loc + eq_offs <= need))
    sel_bf = jnp.where(sel, 1.0, 0.0).astype(BF16)
    wloc, offs = _tile_cumsum(sel_bf, upper, strict_lower)

    sel_tiles = COMB_TILE // LANES
    first_lower = jnp.where(ct < (rt // sel_tiles) * sel_tiles, 1.0, 0.0).astype(BF16)
    offs_first = _dot(first_lower, jnp.broadcast_to(wloc[:, LANES - 1:LANES], wloc.shape).astype(BF16))
    win = jnp.minimum(jnp.floor(offs_first * (1.0 / BF16_SUBLANES)) * BF16_SUBLANES, float(cap - COMB_SPAN))
    win_ref[...] = win
    lp_ref[...] = jnp.where(sel, wloc + offs - 1.0 - win, -1.0)

    tot = jnp.broadcast_to(wloc[:, LANES - 1:LANES], wloc.shape)
    tot_row = _dot_nt(jnp.ones((8, LANES), BF16), sel_bf)
    upper_t = jnp.where(rt <= ct, 1.0, 0.0).astype(BF16)
    incl_row = _dot(tot_row.astype(BF16), upper_t)[0:1, :]
    next_tile = jnp.where(ct == rt + 1, 1.0, 0.0).astype(BF16)
    wloc_step = _dot(next_tile, wloc.astype(BF16)) - wloc
    lane_id = lax.broadcasted_iota(I32, (nt, LANES), 1)
    side = jnp.where(lane_id == 0, 1.0, jnp.where(lane_id == 1, tot, 0.0))
    steps = jnp.concatenate([wloc_step, side], axis=-1).astype(BF16)
    first = jnp.concatenate([wloc[0:1, :], jnp.zeros((1, LANES), F32)], axis=-1)
    for s0 in range(0, cap, SEL_SLOT_CHUNK):
        n = min(SEL_SLOT_CHUNK, cap - s0)
        slot = (lax.broadcasted_iota(I32, (n, 1), 0) + s0).astype(F32)
        before = jnp.where(incl_row <= slot, 1.0, 0.0).astype(BF16)
        got = _dot(before, steps) + first
        tile = got[:, LANES:LANES + 1]
        base = got[:, LANES + 1:LANES + 2]
        lane = jnp.sum(jnp.where(got[:, :LANES] <= slot - base, 1.0, 0.0), axis=-1, keepdims=True)
        tok = jnp.broadcast_to(tile * LANES + lane, (n, LANES))
        idx_ref[:, s0:s0 + n] = tok.T[0:1, :].astype(I32)


def _select_call(aff):
    tokens = aff.shape[0]
    cap = CAPACITY_FACTOR * tokens // N_EXPERTS
    nt = tokens // LANES
    aff_t = aff.T.reshape(N_EXPERTS, nt, LANES)
    blk = pl.BlockSpec((SEL_EXPERTS, nt, LANES), lambda e: (e, 0, 0))
    idx, lp, win = pl.pallas_call(
        functools.partial(_select_kernel, cap=cap),
        grid=(N_EXPERTS // SEL_EXPERTS,),
        in_specs=[blk],
        out_specs=[pl.BlockSpec((SEL_EXPERTS, 1, cap), lambda e: (e, 0, 0)), blk, blk],
        out_shape=[
            jax.ShapeDtypeStruct((N_EXPERTS, 1, cap), I32),
            jax.ShapeDtypeStruct((N_EXPERTS, nt, LANES), F32),
            jax.ShapeDtypeStruct((N_EXPERTS, nt, LANES), F32),
        ],
        compiler_params=_cparams(("parallel",)),
        name="select",
    )(aff_t)
    lp_tok = lp.reshape(N_EXPERTS, tokens).T
    win_start = win[:, ::COMB_TILE // LANES, 0].astype(I32).reshape(-1)
    return idx.reshape(N_EXPERTS * cap), lp_tok, win_start, cap


def _ffn_kernel(idx_ref, x_hbm, wg_ref, wu_ref, wd_ref, ye_ref, land_a, land_b, xe_a, xe_b, sem_ref,
                *, cap):
    e = pl.program_id(0)
    c = pl.program_id(1)
    pairs = cap // (2 * FFN_ROWS)
    step = e * pairs + c
    n_steps = N_EXPERTS * pairs
    last = 2 * n_steps - 1
    land = (land_a, land_b)
    stage = (xe_a, xe_b)

    def row_copy(chunk, r, half):
        tok = idx_ref[chunk * FFN_ROWS + r]
        return pltpu.make_async_copy(x_hbm.at[pl.ds(tok, 1), :], land[half].at[pl.ds(r, 1), :],
                                     sem_ref.at[half])

    def start_chunk(chunk, half):
        for r in range(FFN_ROWS):
            row_copy(chunk, r, half).start()

    def wait_chunk(chunk, half):
        def body(r, carry):
            row_copy(chunk, r, half).wait()
            return carry
        lax.fori_loop(0, FFN_ROWS, body, 0, unroll=8)

    @pl.when(step == 0)
    def _():
        start_chunk(0, 0)
        start_chunk(1, 1)

    for half in range(2):
        wait_chunk(2 * step + half, half)
        stage[half][...] = land[half][...].astype(BF16)
        start_chunk(jnp.minimum(2 * step + 2 + half, last), half)
        xe = stage[half][...]
        gate = _dot(xe, wg_ref[...])
        up = _dot(xe, wu_ref[...])
        hid = (gate * _sigmoid(gate) * up).astype(BF16)
        ye_ref[half * FFN_ROWS:(half + 1) * FFN_ROWS, :] = _dot(hid, wd_ref[...]).astype(BF16)

    @pl.when(step == n_steps - 1)
    def _():
        wait_chunk(last, 0)
        wait_chunk(last, 1)


def _ffn_call(idx, x1, wg, wu, wd, cap):
    pairs = cap // (2 * FFN_ROWS)
    wspec = lambda d0, d1: pl.BlockSpec((None, d0, d1), lambda e, c, idx: (e, 0, 0))
    grid_spec = pltpu.PrefetchScalarGridSpec(
        num_scalar_prefetch=1,
        grid=(N_EXPERTS, pairs),
        in_specs=[
            pl.BlockSpec(memory_space=pl.ANY),
            wspec(D_MODEL, D_EXPERT), wspec(D_MODEL, D_EXPERT), wspec(D_EXPERT, D_MODEL),
        ],
        out_specs=pl.BlockSpec((None, 2 * FFN_ROWS, D_MODEL), lambda e, c, idx: (e, c, 0)),
        scratch_shapes=[
            pltpu.VMEM((FFN_ROWS, D_MODEL), F32),
            pltpu.VMEM((FFN_ROWS, D_MODEL), F32),
            pltpu.VMEM((FFN_ROWS, D_MODEL), BF16),
            pltpu.VMEM((FFN_ROWS, D_MODEL), BF16),
            pltpu.SemaphoreType.DMA((2,)),
        ],
    )
    return pl.pallas_call(
        functools.partial(_ffn_kernel, cap=cap),
        grid_spec=grid_spec,
        out_shape=jax.ShapeDtypeStruct((N_EXPERTS, cap, D_MODEL), BF16),
        compiler_params=_cparams(("arbitrary", "arbitrary")),
        name="ffn",
    )(idx, x1, wg, wu, wd)


def _combine_kernel(win_ref, x1_ref, p_ref, lp_ref, aff_ref, ye_hbm, wpg_ref, bpg_ref, wple_ref,
                    g_ref, b_ref, o_ref, buf_ref, xbuf_ref, acc_ref, sem_ref, xsem_ref, *, n_tiles):
    i = pl.program_id(0)
    slot = i % 2
    width = N_EXPERTS * COMB_WIN

    def win_copy(tile, e, rnd, dst, sem):
        start = pl.multiple_of(win_ref[e * n_tiles + tile] + rnd * COMB_WIN, BF16_SUBLANES)
        return pltpu.make_async_copy(ye_hbm.at[e, pl.ds(start, COMB_WIN), :],
                                     dst.at[pl.ds(e * COMB_WIN, COMB_WIN), :], sem)

    def start_tile(tile, buf_slot):
        for e in range(N_EXPERTS):
            win_copy(tile, e, 0, buf_ref.at[buf_slot], sem_ref.at[buf_slot]).start()

    @pl.when(i == 0)
    def _():
        start_tile(0, 0)

    @pl.when(i + 1 < n_tiles)
    def _():
        start_tile(i + 1, 1 - slot)

    x1 = x1_ref[...]
    ple = _sigmoid(_dot(x1.astype(BF16), wpg_ref[...]) + bpg_ref[...]) \
        * _dot(p_ref[...].astype(BF16), wple_ref[...])
    acc_ref[...] = DN_ALPHA * x1 + ple

    lp = lp_ref[...]
    rnd_of = jnp.floor(lp * (1.0 / COMB_WIN))
    row_of = lp - rnd_of * COMB_WIN
    er = lax.broadcasted_iota(I32, (N_EXPERTS, width), 0)
    ec = lax.broadcasted_iota(I32, (N_EXPERTS, width), 1) // COMB_WIN
    expand = jnp.where(er == ec, 1.0, 0.0).astype(BF16)
    lane_row = (lax.broadcasted_iota(I32, (COMB_TILE, width), 1) % COMB_WIN).astype(F32)
    hit = _dot(row_of.astype(BF16), expand) == lane_row
    rnd_e = _dot(rnd_of.astype(BF16), expand)
    gate_e = jnp.where(hit, _dot(aff_ref[...].astype(BF16), expand), 0.0)

    def add_round(rnd, rows):
        lhs = jnp.where(rnd_e == rnd, gate_e, 0.0).astype(BF16)
        acc_ref[...] += _dot(lhs, rows)

    for e in range(N_EXPERTS):
        win_copy(i, e, 0, buf_ref.at[slot], sem_ref.at[slot]).wait()
    add_round(0.0, buf_ref[slot])

    n_rounds = jnp.max(rnd_of).astype(I32) + 1
    for rnd in range(1, COMB_ROUNDS):
        @pl.when(rnd < n_rounds)
        def _():
            for e in range(N_EXPERTS):
                win_copy(i, e, rnd, xbuf_ref, xsem_ref.at[0]).start()
            for e in range(N_EXPERTS):
                win_copy(i, e, rnd, xbuf_ref, xsem_ref.at[0]).wait()
            add_round(float(rnd), xbuf_ref[...])

    o_ref[...] = _layer_norm_rows(acc_ref[...], g_ref[...], b_ref[...])


def _combine_call(win_start, x1, p2d, lp_tok, aff, ye, wpg, bpg, wple, ln_g, ln_b):
    tokens = x1.shape[0]
    n_tiles = tokens // COMB_TILE
    row = lambda i, w: (i, 0)
    fixed = lambda i, w: (0, 0)
    vec = pl.BlockSpec((1, D_MODEL), fixed)
    grid_spec = pltpu.PrefetchScalarGridSpec(
        num_scalar_prefetch=1,
        grid=(n_tiles,),
        in_specs=[
            pl.BlockSpec((COMB_TILE, D_MODEL), row),
            pl.BlockSpec((COMB_TILE, PLE_DIM), row),
            pl.BlockSpec((COMB_TILE, N_EXPERTS), row),
            pl.BlockSpec((COMB_TILE, N_EXPERTS), row),
            pl.BlockSpec(memory_space=pl.ANY),
            pl.BlockSpec((D_MODEL, D_MODEL), fixed),
            vec,
            pl.BlockSpec((PLE_DIM, D_MODEL), fixed),
            vec, vec,
        ],
        out_specs=pl.BlockSpec((COMB_TILE, D_MODEL), row),
        scratch_shapes=[
            pltpu.VMEM((2, N_EXPERTS * COMB_WIN, D_MODEL), BF16),
            pltpu.VMEM((N_EXPERTS * COMB_WIN, D_MODEL), BF16),
            pltpu.VMEM((COMB_TILE, D_MODEL), F32),
            pltpu.SemaphoreType.DMA((2,)),
            pltpu.SemaphoreType.DMA((1,)),
        ],
    )
    return pl.pallas_call(
        functools.partial(_combine_kernel, n_tiles=n_tiles),
        grid_spec=grid_spec,
        out_shape=jax.ShapeDtypeStruct((tokens, D_MODEL), F32),
        compiler_params=_cparams(("arbitrary",)),
        name="combine",
    )(win_start, x1, p2d, lp_tok, aff, ye, wpg, bpg, wple, ln_g, ln_b)


def _prep_weights(ln_in_g, ln_in_b, w_in, q_norm_w, k_norm_w, conv_w, conv_b, dt_bias, a_log,
                  d_skip, ssd_norm_w, w_out, ln1_g, ln1_b, w_router, w_e_gate, w_e_up, w_e_down,
                  w_ple_gate, b_ple_gate, w_ple, ln2_g, ln2_b):
    l = 0
    vec = lambda a: a.reshape(1, -1)
    w_in_pad = jnp.zeros((D_MODEL, IN_COLS_PAD), F32).at[:, :IN_COLS].set(w_in[l]).astype(BF16)
    wr = jnp.zeros((D_MODEL, LANES), F32).at[:, :N_EXPERTS].set(w_router[l])
    wr_hi, wr_lo = _split_bf16(wr)
    return dict(
        ln_in_g=vec(ln_in_g), ln_in_b=vec(ln_in_b), w_in=w_in_pad,
        qw=vec(jnp.tile(q_norm_w[l], N_Q_HEADS)), kw=vec(jnp.tile(k_norm_w[l], N_KV_HEADS)),
        conv_w=conv_w[l], conv_b=conv_b[l], dt_bias=dt_bias[l], a_log=a_log[l], d_skip=d_skip[l],
        norm_w=vec(ssd_norm_w[l]), w_out=w_out[l].astype(BF16), ln1_g=vec(ln1_g[l]), ln1_b=vec(ln1_b[l]),
        wr_hi=wr_hi, wr_lo=wr_lo,
        wg=w_e_gate[l].astype(BF16), wu=w_e_up[l].astype(BF16), wd=w_e_down[l].astype(BF16),
        wpg=w_ple_gate[l].astype(BF16), bpg=vec(b_ple_gate[l]), wple=w_ple[l].astype(BF16),
        ln2_g=vec(ln2_g[l]), ln2_b=vec(ln2_b[l]),
    )


def _trunk(x, p, w):
    batch, seq, _ = x.shape
    tokens = batch * seq
    x2d = x.reshape(tokens, D_MODEL)
    q, k, v, z, xc, dt2 = _inproj_call(x2d, seq, w["ln_in_g"], w["ln_in_b"], w["w_in"], w["qw"], w["kw"],
                                       w["conv_w"], w["conv_b"])
    attn = _attn_call(q, k, v, batch, seq)
    y2 = _ssd_call(xc, dt2, batch, seq, w["dt_bias"], w["a_log"], w["d_skip"])
    x1, aff = _mix_call(x2d, w["ln_in_g"], w["ln_in_b"], attn, y2, z, w["norm_w"], w["w_out"],
                        w["ln1_g"], w["ln1_b"], w["wr_hi"], w["wr_lo"])
    idx, lp_tok, win_start, cap = _select_call(aff)
    ye = _ffn_call(idx, x1, w["wg"], w["wu"], w["wd"], cap)
    out = _combine_call(win_start, x1, p[0].reshape(tokens, PLE_DIM), lp_tok, aff, ye,
                        w["wpg"], w["bpg"], w["wple"], w["ln2_g"], w["ln2_b"])
    return out.reshape(batch, seq, D_MODEL)


def kernel(x_prompt, x_sample, p_prompt, p_sample, ln_in_g, ln_in_b, w_in, q_norm_w, k_norm_w, conv_w, conv_b, dt_bias, a_log, d_skip, ssd_norm_w, w_out, ln1_g, ln1_b, w_router, w_e_gate, w_e_up, w_e_down, w_ple_gate, b_ple_gate, w_ple, ln2_g, ln2_b):
    w = _prep_weights(ln_in_g, ln_in_b, w_in, q_norm_w, k_norm_w, conv_w, conv_b, dt_bias, a_log,
                      d_skip, ssd_norm_w, w_out, ln1_g, ln1_b, w_router, w_e_gate, w_e_up, w_e_down,
                      w_ple_gate, b_ple_gate, w_ple, ln2_g, ln2_b)
    return (_trunk(x_prompt, p_prompt, w), _trunk(x_sample, p_sample, w))
```

```python
import functools
import math

import jax
import jax.numpy as jnp
from jax import lax
from jax.experimental import pallas as pl
from jax.experimental.pallas import tpu as pltpu

F32 = jnp.float32
BF16 = jnp.bfloat16
I32 = jnp.int32

D_MODEL = 1024
DEPTH = 1
GRID_W = 64
N_Q_HEADS = 8
N_KV_HEADS = 2
HEAD_DIM = 64
Q_PER_KV = N_Q_HEADS // N_KV_HEADS
ATTN_W = N_Q_HEADS * HEAD_DIM
KV_W = N_KV_HEADS * HEAD_DIM
ROPE_THETA = 10000.0
SSD_HEADS = 8
SSD_HEAD_DIM = 64
D_INNER = SSD_HEADS * SSD_HEAD_DIM
SSD_GROUPS = 2
HEADS_PER_GROUP = SSD_HEADS // SSD_GROUPS
D_STATE = 128
D_CONV = 5
CHUNK = 128
CONV_CH = D_INNER + 2 * SSD_GROUPS * D_STATE
K0 = ATTN_W
V0 = K0 + KV_W
Z0 = V0 + KV_W
XBC0 = Z0 + D_INNER
DT0 = XBC0 + CONV_CH
IN_COLS = DT0 + 2 * SSD_HEADS
N_EXPERTS = 16
CAPACITY_FACTOR = 2
D_EXPERT = 1024
PLE_DIM = 256
DN_ALPHA = (2.0 * DEPTH) ** 0.25
NORM_EPS = 1e-6
LN_EPS = 1e-5

LANES = 128
BF16_SUBLANES = 16
IN_COLS_PAD = DT0 + LANES
TOKEN_TILE = 1024
ROW_GROUPS = 4
HALO = 8
SSD_BLOCK = 16
ATTN_Q_TILE = 1024
ATTN_Q_SUB = 256
ATTN_V_ROWS = HEAD_DIM + BF16_SUBLANES
FFN_ROWS = 256
COMB_TILE = 256
COMB_WIN = 64
COMB_ROUNDS = -(-(COMB_TILE + BF16_SUBLANES - 1) // COMB_WIN)
COMB_SPAN = COMB_ROUNDS * COMB_WIN
SEL_SLOT_CHUNK = 1024
SEL_EXPERTS = 2
VMEM_LIMIT = 56 * 1024 * 1024


def _cparams(sem):
    return pltpu.CompilerParams(dimension_semantics=sem, vmem_limit_bytes=VMEM_LIMIT)


def _layer_norm_rows(x, g, b):
    xc = x - jnp.mean(x, axis=-1, keepdims=True)
    return xc * lax.rsqrt(jnp.mean(xc * xc, axis=-1, keepdims=True) + LN_EPS) * g + b


def _sigmoid(x):
    return 1.0 / (1.0 + jnp.exp(-x))


def _dot(a, b):
    return jnp.dot(a, b, preferred_element_type=F32)


def _dot_nt(a, b):
    return lax.dot_general(a, b, (((1,), (1,)), ((), ())), preferred_element_type=F32)


def _split_bf16(x):
    hi = x.astype(BF16)
    lo = (x - hi.astype(F32)).astype(BF16)
    return hi, lo


def _split3_bf16(x):
    hi = x.astype(BF16)
    rest = x - hi.astype(F32)
    mid = rest.astype(BF16)
    lo = (rest - mid.astype(F32)).astype(BF16)
    return hi, mid, lo


def _rope_swap(x):
    outs = []
    for c in range(x.shape[-1] // LANES):
        xc = x[:, c * LANES:(c + 1) * LANES]
        lane = lax.broadcasted_iota(I32, xc.shape, 1)
        up = pltpu.roll(xc, LANES - 16, axis=1)
        dn = pltpu.roll(xc, 16, axis=1)
        outs.append(jnp.where((lane % 32) < 16, up, dn))
    return outs[0] if len(outs) == 1 else jnp.concatenate(outs, axis=-1)


def _head_rms(x, w):
    width = x.shape[-1]
    r = lax.broadcasted_iota(I32, (width, width), 0) // HEAD_DIM
    c = lax.broadcasted_iota(I32, (width, width), 1) // HEAD_DIM
    blockdiag = jnp.where(r == c, 1.0, 0.0).astype(BF16)
    hi, lo = _split_bf16(x * x)
    ssq = _dot(hi, blockdiag) + _dot(lo, blockdiag)
    return x * lax.rsqrt(ssq * (1.0 / HEAD_DIM) + NORM_EPS) * w


def _inproj_kernel(x_ref, xprev_ref, xnext_ref, g_ref, b_ref, w_ref, qw_ref, kw_ref, cos_ref, sin_ref,
                   cw_ref, cb_ref, q_ref, k_ref, v_ref, z_ref, xc_ref, dt_ref, pad_ref, *, tiles_per_seq):
    tm = x_ref.shape[0]
    sub = tm // ROW_GROUPS

    pos = pl.program_id(0) % tiles_per_seq
    has_prev = (pos > 0).astype(F32)
    has_next = (pos < tiles_per_seq - 1).astype(F32)
    edge = jnp.concatenate([xprev_ref[...], xnext_ref[...]], axis=0)
    edge = _dot(_layer_norm_rows(edge, g_ref[...], b_ref[...]).astype(BF16), w_ref[:, XBC0:DT0])
    pad_ref[0:HALO, :] = edge[:HALO] * has_prev
    pad_ref[HALO + tm:, :] = edge[HALO:] * has_next

    for grp in range(ROW_GROUPS):
        rs = slice(grp * sub, (grp + 1) * sub)
        xln = _layer_norm_rows(x_ref[rs, :], g_ref[...], b_ref[...])
        proj = _dot(xln.astype(BF16), w_ref[...])
        cos = cos_ref[rs, :]
        sin = sin_ref[rs, :]
        q = _head_rms(proj[:, :K0], qw_ref[...])
        cos_q = jnp.concatenate([cos] * (ATTN_W // LANES), axis=-1)
        sin_q = jnp.concatenate([sin] * (ATTN_W // LANES), axis=-1)
        q = (q * cos_q + _rope_swap(q) * sin_q) * (HEAD_DIM ** -0.5)
        q_ref[rs, :] = q.astype(BF16)
        k = _head_rms(proj[:, K0:V0], kw_ref[...])
        k = (k * cos + _rope_swap(k) * sin).astype(BF16)
        v = proj[:, V0:Z0]
        ones_col = jnp.where(lax.broadcasted_iota(I32, (sub, LANES - HEAD_DIM), 1) == 0, 1.0, 0.0)
        for g in range(N_KV_HEADS):
            k_ref[g, rs, :] = k[:, g * HEAD_DIM:(g + 1) * HEAD_DIM]
            v_ext = jnp.concatenate([v[:, g * HEAD_DIM:(g + 1) * HEAD_DIM], ones_col], axis=-1)
            v_ref[g, :, rs] = v_ext.T.astype(BF16)
        z_ref[rs, :] = proj[:, Z0:XBC0]
        pad_ref[HALO + grp * sub:HALO + (grp + 1) * sub, :] = proj[:, XBC0:DT0]
        dt = proj[:, DT0:DT0 + LANES]
        for d in range(2):
            dt_ref[d, rs, :] = dt[:, d * SSD_HEADS:(d + 1) * SSD_HEADS]

    cw = cw_ref[...]
    for grp in range(ROW_GROUPS):
        window = pad_ref[grp * sub:grp * sub + sub + 2 * HALO, :]
        acc = jnp.broadcast_to(cb_ref[...], (sub, CONV_CH))
        for tap in range(D_CONV):
            shift = (D_CONV // 2 - tap) % (sub + 2 * HALO)
            shifted = window if shift == 0 else pltpu.roll(window, shift, axis=0)
            acc = acc + shifted[HALO:HALO + sub, :] * cw[tap:tap + 1, :]
        xc_ref[grp * sub:(grp + 1) * sub, :] = acc * _sigmoid(acc)


def _rope_tables(seq):
    rows = seq // GRID_W
    row_id = jnp.broadcast_to(jnp.arange(rows)[:, None], (rows, GRID_W)).reshape(seq)
    col_id = jnp.broadcast_to(jnp.arange(GRID_W)[None, :], (rows, GRID_W)).reshape(seq)
    pos = jnp.stack([row_id, col_id], axis=-1).astype(F32)
    n_freq = HEAD_DIM // 4
    inv_freq = ROPE_THETA ** (-jnp.arange(n_freq, dtype=F32) / n_freq)
    ang = pos[:, :, None] * inv_freq
    cos = jnp.broadcast_to(jnp.cos(ang)[:, :, None, :], (seq, 2, 2, n_freq)).reshape(seq, HEAD_DIM)
    sin = jnp.broadcast_to(jnp.sin(ang)[:, :, None, :], (seq, 2, 2, n_freq))
    sin = (sin * jnp.array([-1.0, 1.0], F32)[None, None, :, None]).reshape(seq, HEAD_DIM)
    reps = LANES // HEAD_DIM
    return jnp.tile(cos, (1, reps)), jnp.tile(sin, (1, reps))


def _inproj_call(x2d, seq, ln_g, ln_b, w_in_pad, qw, kw, conv_w, conv_b):
    tokens = x2d.shape[0]
    tm = TOKEN_TILE
    nt = tokens // tm
    tiles_per_seq = seq // tm
    halo_per_tile = tm // HALO
    n_halo = tokens // HALO
    cos, sin = _rope_tables(seq)
    cw = jnp.zeros((8, CONV_CH), F32).at[:D_CONV].set(conv_w)
    cb = conv_b.reshape(1, CONV_CH)
    row = lambda i: (i, 0)
    fixed = lambda i: (0, 0)
    return pl.pallas_call(
        functools.partial(_inproj_kernel, tiles_per_seq=tiles_per_seq),
        grid=(nt,),
        in_specs=[
            pl.BlockSpec((tm, D_MODEL), row),
            pl.BlockSpec((HALO, D_MODEL), lambda i: (jnp.maximum(i * halo_per_tile - 1, 0), 0)),
            pl.BlockSpec((HALO, D_MODEL), lambda i: (jnp.minimum((i + 1) * halo_per_tile, n_halo - 1), 0)),
            pl.BlockSpec((1, D_MODEL), fixed),
            pl.BlockSpec((1, D_MODEL), fixed),
            pl.BlockSpec((D_MODEL, IN_COLS_PAD), fixed),
            pl.BlockSpec((1, ATTN_W), fixed),
            pl.BlockSpec((1, KV_W), fixed),
            pl.BlockSpec((tm, LANES), lambda i: (i % tiles_per_seq, 0)),
            pl.BlockSpec((tm, LANES), lambda i: (i % tiles_per_seq, 0)),
            pl.BlockSpec((8, CONV_CH), fixed),
            pl.BlockSpec((1, CONV_CH), fixed),
        ],
        out_specs=[
            pl.BlockSpec((tm, ATTN_W), row),
            pl.BlockSpec((N_KV_HEADS, tm, HEAD_DIM), lambda i: (0, i, 0)),
            pl.BlockSpec((N_KV_HEADS, LANES, tm), lambda i: (0, 0, i)),
            pl.BlockSpec((tm, D_INNER), row),
            pl.BlockSpec((tm, CONV_CH), row),
            pl.BlockSpec((2, tm, SSD_HEADS), lambda i: (0, i, 0)),
        ],
        out_shape=[
            jax.ShapeDtypeStruct((tokens, ATTN_W), BF16),
            jax.ShapeDtypeStruct((N_KV_HEADS, tokens, HEAD_DIM), BF16),
            jax.ShapeDtypeStruct((N_KV_HEADS, LANES, tokens), BF16),
            jax.ShapeDtypeStruct((tokens, D_INNER), F32),
            jax.ShapeDtypeStruct((tokens, CONV_CH), F32),
            jax.ShapeDtypeStruct((2, tokens, SSD_HEADS), F32),
        ],
        scratch_shapes=[pltpu.VMEM((tm + 2 * HALO, CONV_CH), F32)],
        compiler_params=_cparams(("parallel",)),
        name="inproj",
    )(x2d, x2d, x2d, ln_g, ln_b, w_in_pad, qw, kw, cos, sin, cw, cb)


def _attn_kernel(q_ref, k_ref, vt_ref, o_ref):
    k = k_ref[...]
    vt = vt_ref[0:ATTN_V_ROWS, :]
    tq = ATTN_Q_SUB
    for t in range(q_ref.shape[0] // tq):
        q = q_ref[t * tq:(t + 1) * tq, :]
        qs = jnp.concatenate([q[:, h * HEAD_DIM:(h + 1) * HEAD_DIM] for h in range(Q_PER_KV)], axis=0)
        s_all = _dot_nt(qs, k)
        outs = []
        for h in range(Q_PER_KV):
            s = s_all[h * tq:(h + 1) * tq]
            m = jnp.max(s, axis=-1, keepdims=True)
            p = jnp.exp((s - m).astype(BF16))
            o = _dot_nt(vt, p).T
            outs.append(o[:, :HEAD_DIM] / o[:, HEAD_DIM:HEAD_DIM + 1])
        o_ref[t * tq:(t + 1) * tq, :] = jnp.concatenate(outs, axis=-1).astype(BF16)


def _attn_call(q, k, v, batch, seq):
    tokens = batch * seq
    tq = ATTN_Q_TILE
    nq = seq // tq
    width = Q_PER_KV * HEAD_DIM
    return pl.pallas_call(
        _attn_kernel,
        grid=(batch, N_KV_HEADS, nq),
        in_specs=[
            pl.BlockSpec((tq, width), lambda b, g, i: (b * nq + i, g)),
            pl.BlockSpec((None, seq, HEAD_DIM), lambda b, g, i: (g, b, 0)),
            pl.BlockSpec((None, LANES, seq), lambda b, g, i: (g, 0, b)),
        ],
        out_specs=pl.BlockSpec((tq, width), lambda b, g, i: (b * nq + i, g)),
        out_shape=jax.ShapeDtypeStruct((tokens, ATTN_W), BF16),
        compiler_params=_cparams(("parallel", "parallel", "parallel")),
        name="attn",
    )(q, k, v)


def _ssd_kernel(xc_ref, dt_ref, dtt_ref, dtb_ref, dtbt_ref, alog_ref, alogt_ref, dskip_ref,
                y_ref, state_ref):
    fwd = pl.program_id(1) == 0

    @pl.when(pl.program_id(2) == 0)
    def _():
        state_ref[...] = jnp.zeros_like(state_ref)

    for j in range(SSD_BLOCK):
        sub = jnp.where(fwd, j, SSD_BLOCK - 1 - j)
        _ssd_chunk(pl.multiple_of(sub * CHUNK, CHUNK), fwd, xc_ref, dt_ref, dtt_ref, dtb_ref, dtbt_ref,
                   alog_ref, alogt_ref, dskip_ref, y_ref, state_ref)


def _ssd_chunk(row0, fwd, xc_ref, dt_ref, dtt_ref, dtb_ref, dtbt_ref, alog_ref, alogt_ref, dskip_ref,
               y_ref, state_ref):
    rows = pl.ds(row0, CHUNK)
    xc = xc_ref[rows, :]
    xs = xc[:, :D_INNER]
    b_all = xc[:, D_INNER:D_INNER + SSD_GROUPS * D_STATE]
    c_all = xc[:, D_INNER + SSD_GROUPS * D_STATE:]

    dt = jax.nn.softplus(dt_ref[rows, :] + dtb_ref[...])
    a = dt * -jnp.exp(alog_ref[...])
    a_row = jax.nn.softplus(dtt_ref[:, rows] + dtbt_ref[...]) * -jnp.exp(alogt_ref[...])

    ri = lax.broadcasted_iota(I32, (CHUNK, CHUNK), 0)
    ci = lax.broadcasted_iota(I32, (CHUNK, CHUNK), 1)
    ahead = jnp.where(fwd, ri - ci, ci - ri)
    causal = ahead >= 0
    tri = jnp.where(causal, 1.0, 0.0).astype(BF16)
    tri_t = jnp.where(ahead <= 0, 1.0, 0.0).astype(BF16)
    cum = sum(_dot(tri, piece) for piece in _split3_bf16(a))
    cum_row = sum(_dot(piece, tri_t) for piece in _split3_bf16(a_row))
    total = jnp.sum(a, axis=0, keepdims=True)

    er = lax.broadcasted_iota(I32, (SSD_HEADS, D_INNER), 0)
    ec = lax.broadcasted_iota(I32, (SSD_HEADS, D_INNER), 1) // SSD_HEAD_DIM
    expand = jnp.where(er == ec, 1.0, 0.0).astype(BF16)
    stacked = jnp.concatenate([dt, jnp.exp(total - cum), jnp.exp(cum)], axis=0)
    stacked_e = sum(_dot(piece, expand) for piece in _split_bf16(stacked))
    dt_e = stacked_e[:CHUNK]
    dec_state_e = stacked_e[CHUNK:2 * CHUNK]
    dec_out_e = stacked_e[2 * CHUNK:]
    dec_chunk_e = sum(_dot(piece, expand) for piece in _split_bf16(jnp.exp(total)))

    xdt = xs * dt_e
    xdt_bf = xdt.astype(BF16)
    xdec_bf = (xdt * dec_state_e).astype(BF16)

    ys = []
    for g in range(SSD_GROUPS):
        bg = b_all[:, g * D_STATE:(g + 1) * D_STATE]
        cg = c_all[:, g * D_STATE:(g + 1) * D_STATE].astype(BF16)
        cb = _dot_nt(cg, bg.astype(BF16))
        lo = g * HEADS_PER_GROUP * SSD_HEAD_DIM
        hi = lo + HEADS_PER_GROUP * SSD_HEAD_DIM
        prev = state_ref[g]
        y_off = _dot(cg, prev.astype(BF16)) * dec_out_e[:, lo:hi]
        y_heads = []
        for r in range(HEADS_PER_GROUP):
            h = g * HEADS_PER_GROUP + r
            seg = cum[:, h:h + 1] - cum_row[h:h + 1, :]
            m = jnp.where(causal, cb * jnp.exp(jnp.where(causal, seg, 0.0)), 0.0)
            y_heads.append(_dot(m.astype(BF16), xdt_bf[:, h * SSD_HEAD_DIM:(h + 1) * SSD_HEAD_DIM]))
        ys.append(jnp.concatenate(y_heads, axis=-1) + y_off)
        st = _dot(bg.T.astype(BF16), xdec_bf[:, lo:hi])
        state_ref[g] = prev * dec_chunk_e[:, lo:hi] + st
    y = jnp.concatenate(ys, axis=-1)
    y_ref[rows, :] = y + jnp.where(fwd, 1.0, 0.0) * (dskip_ref[...] * xs)


def _ssd_call(xc, dt2, batch, seq, dt_bias, a_log, d_skip):
    tokens = batch * seq
    rows = SSD_BLOCK * CHUNK
    nc = seq // rows
    dtt = dt2.reshape(2, batch, seq, SSD_HEADS).transpose(1, 0, 3, 2).reshape(batch * 2 * SSD_HEADS, seq)
    dtb = dt_bias.reshape(2, 1, SSD_HEADS)
    dtbt = dt_bias.reshape(2, SSD_HEADS, 1)
    na = a_log.reshape(2, 1, SSD_HEADS)
    nat = a_log.reshape(2, SSD_HEADS, 1)
    dskip_e = jnp.repeat(d_skip, SSD_HEAD_DIM).reshape(1, D_INNER)

    def ceff(d, c):
        return jnp.where(d == 0, c, nc - 1 - c)

    def main_idx(b, d, c):
        return (b * nc + ceff(d, c), 0)

    sel_d = lambda b, d, c: (d, 0, 0)
    fixed = lambda b, d, c: (0, 0)
    return pl.pallas_call(
        _ssd_kernel,
        grid=(batch, 2, nc),
        in_specs=[
            pl.BlockSpec((rows, CONV_CH), main_idx),
            pl.BlockSpec((None, rows, SSD_HEADS), lambda b, d, c: (d, b * nc + ceff(d, c), 0)),
            pl.BlockSpec((SSD_HEADS, rows), lambda b, d, c: (b * 2 + d, ceff(d, c))),
            pl.BlockSpec((None, 1, SSD_HEADS), sel_d),
            pl.BlockSpec((None, SSD_HEADS, 1), sel_d),
            pl.BlockSpec((None, 1, SSD_HEADS), sel_d),
            pl.BlockSpec((None, SSD_HEADS, 1), sel_d),
            pl.BlockSpec((1, D_INNER), fixed),
        ],
        out_specs=pl.BlockSpec((None, rows, D_INNER), lambda b, d, c: (d, b * nc + ceff(d, c), 0)),
        out_shape=jax.ShapeDtypeStruct((2, tokens, D_INNER), F32),
        scratch_shapes=[
            pltpu.VMEM((SSD_GROUPS, D_STATE, HEADS_PER_GROUP * SSD_HEAD_DIM), F32),
        ],
        compiler_params=_cparams(("parallel", "arbitrary", "arbitrary")),
        name="ssd",
    )(xc, dt2, dtt, dtb, dtbt, na, nat, dskip_e)


def _mix_kernel(x_ref, ing_ref, inb_ref, attn_ref, yf_ref, yb_ref, z_ref, nw_ref, wo_ref,
                g_ref, b_ref, wrh_ref, wrl_ref, x1_ref, aff_ref):
    sub = x_ref.shape[0] // ROW_GROUPS
    for grp in range(ROW_GROUPS):
        rs = slice(grp * sub, (grp + 1) * sub)
        xln = _layer_norm_rows(x_ref[rs, :], ing_ref[...], inb_ref[...])
        z = z_ref[rs, :]
        y = (yf_ref[rs, :] + yb_ref[rs, :]) * (z * _sigmoid(z))
        gw = D_INNER // SSD_GROUPS
        parts = []
        for g in range(SSD_GROUPS):
            yg = y[:, g * gw:(g + 1) * gw]
            parts.append(yg * lax.rsqrt(jnp.mean(yg * yg, axis=-1, keepdims=True) + NORM_EPS))
        ssd = (jnp.concatenate(parts, axis=-1) * nw_ref[...]).astype(BF16)
        mix = _dot(jnp.concatenate([attn_ref[rs, :], ssd], axis=-1), wo_ref[...])
        x1 = _layer_norm_rows(DN_ALPHA * xln + mix, g_ref[...], b_ref[...])
        x1_ref[rs, :] = x1
        hi, lo = _split_bf16(x1)
        wrh = wrh_ref[...]
        logits = _dot(hi, wrh) + _dot(lo, wrh) + _dot(hi, wrl_ref[...])
        lane = lax.broadcasted_iota(I32, logits.shape, 1)
        logits = jnp.where(lane < N_EXPERTS, logits, -jnp.inf)
        e = jnp.exp(logits - jnp.max(logits, axis=-1, keepdims=True))
        aff = e / jnp.sum(e, axis=-1, keepdims=True)
        aff_ref[rs, :] = aff[:, :N_EXPERTS]


def _mix_call(x2d, ln_in_g, ln_in_b, attn, y2, z, norm_w, w_out, ln_g, ln_b, wr_hi, wr_lo):
    tokens = x2d.shape[0]
    tm = TOKEN_TILE
    row = lambda i: (i, 0)
    fixed = lambda i: (0, 0)
    vec = pl.BlockSpec((1, D_MODEL), fixed)
    return pl.pallas_call(
        _mix_kernel,
        grid=(tokens // tm,),
        in_specs=[
            pl.BlockSpec((tm, D_MODEL), row), vec, vec,
            pl.BlockSpec((tm, ATTN_W), row),
            pl.BlockSpec((None, tm, D_INNER), lambda i: (0, i, 0)),
            pl.BlockSpec((None, tm, D_INNER), lambda i: (1, i, 0)),
            pl.BlockSpec((tm, D_INNER), row),
            pl.BlockSpec((1, D_INNER), fixed),
            pl.BlockSpec((D_MODEL, D_MODEL), fixed),
            vec, vec,
            pl.BlockSpec((D_MODEL, LANES), fixed),
            pl.BlockSpec((D_MODEL, LANES), fixed),
        ],
        out_specs=[
            pl.BlockSpec((tm, D_MODEL), row),
            pl.BlockSpec((tm, N_EXPERTS), row),
        ],
        out_shape=[
            jax.ShapeDtypeStruct((tokens, D_MODEL), F32),
            jax.ShapeDtypeStruct((tokens, N_EXPERTS), F32),
        ],
        compiler_params=_cparams(("parallel",)),
        name="mix",
    )(x2d, ln_in_g, ln_in_b, attn, y2, y2, z, norm_w, w_out, ln_g, ln_b, wr_hi, wr_lo)


def _tile_cumsum(mask_bf, upper, strict_lower):
    wloc = _dot(mask_bf, upper)
    tot = jnp.broadcast_to(wloc[:, LANES - 1:LANES], wloc.shape).astype(BF16)
    offs = _dot(strict_lower, tot)
    return wloc, offs


def _select_kernel(aff_ref, idx_ref, lp_ref, win_ref, *, cap):
    for e in range(SEL_EXPERTS):
        _select_expert(aff_ref.at[e], idx_ref.at[e], lp_ref.at[e], win_ref.at[e], cap)


def _select_expert(aff_ref, idx_ref, lp_ref, win_ref, cap):
    nt = aff_ref.shape[0]
    aff = aff_ref[...]

    def enough(cand_bits):
        cnt = jnp.sum(jnp.where(aff >= pltpu.bitcast(cand_bits, F32), 1.0, 0.0), keepdims=True)
        return cnt >= cap

    t_bits = jnp.where(enough(jnp.full((1, 1), 1 << 30, I32)), 1 << 30, 0).astype(I32)
    for bit in range(29, 0, -2):
        hi, lo = 1 << bit, 1 << (bit - 1)
        c1, c2, c3 = t_bits | lo, t_bits | hi, t_bits | hi | lo
        t_bits = jnp.where(enough(c3), c3, jnp.where(enough(c2), c2, jnp.where(enough(c1), c1, t_bits)))
    t = pltpu.bitcast(t_bits, F32)
    gt = aff > t
    eq = aff == t
    need = cap - jnp.sum(jnp.where(gt, 1.0, 0.0), keepdims=True)

    ri = lax.broadcasted_iota(I32, (LANES, LANES), 0)
    ci = lax.broadcasted_iota(I32, (LANES, LANES), 1)
    upper = jnp.where(ri <= ci, 1.0, 0.0).astype(BF16)
    rt = lax.broadcasted_iota(I32, (nt, nt), 0)
    ct = lax.broadcasted_iota(I32, (nt, nt), 1)
    strict_lower = jnp.where(ct < rt, 1.0, 0.0).astype(BF16)

    eq_loc, eq_offs = _tile_cumsum(jnp.where(eq, 1.0, 0.0).astype(BF16), upper, strict_lower)
    sel = gt | (eq & (eq_loc + eq_offs <= need))
    sel_bf = jnp.where(sel, 1.0, 0.0).astype(BF16)
    wloc, offs = _tile_cumsum(sel_bf, upper, strict_lower)

    sel_tiles = COMB_TILE // LANES
    first_lower = jnp.where(ct < (rt // sel_tiles) * sel_tiles, 1.0, 0.0).astype(BF16)
    offs_first = _dot(first_lower, jnp.broadcast_to(wloc[:, LANES - 1:LANES], wloc.shape).astype(BF16))
    win = jnp.minimum(jnp.floor(offs_first * (1.0 / BF16_SUBLANES)) * BF16_SUBLANES, float(cap - COMB_SPAN))
    win_ref[...] = win
    lp_ref[...] = jnp.where(sel, wloc + offs - 1.0 - win, -1.0)

    tot = jnp.broadcast_to(wloc[:, LANES - 1:LANES], wloc.shape)
    tot_row = _dot_nt(jnp.ones((8, LANES), BF16), sel_bf)
    upper_t = jnp.where(rt <= ct, 1.0, 0.0).astype(BF16)
    incl_row = _dot(tot_row.astype(BF16), upper_t)[0:1, :]
    next_tile = jnp.where(ct == rt + 1, 1.0, 0.0).astype(BF16)
    wloc_step = _dot(next_tile, wloc.astype(BF16)) - wloc
    lane_id = lax.broadcasted_iota(I32, (nt, LANES), 1)
    side = jnp.where(lane_id == 0, 1.0, jnp.where(lane_id == 1, tot, 0.0))
    steps = jnp.concatenate([wloc_step, side], axis=-1).astype(BF16)
    first = jnp.concatenate([wloc[0:1, :], jnp.zeros((1, LANES), F32)], axis=-1)
    for s0 in range(0, cap, SEL_SLOT_CHUNK):
        n = min(SEL_SLOT_CHUNK, cap - s0)
        slot = (lax.broadcasted_iota(I32, (n, 1), 0) + s0).astype(F32)
        before = jnp.where(incl_row <= slot, 1.0, 0.0).astype(BF16)
        got = _dot(before, steps) + first
        tile = got[:, LANES:LANES + 1]
        base = got[:, LANES + 1:LANES + 2]
        lane = jnp.sum(jnp.where(got[:, :LANES] <= slot - base, 1.0, 0.0), axis=-1, keepdims=True)
        tok = jnp.broadcast_to(tile * LANES + lane, (n, LANES))
        idx_ref[:, s0:s0 + n] = tok.T[0:1, :].astype(I32)


def _select_call(aff):
    tokens = aff.shape[0]
    cap = CAPACITY_FACTOR * tokens // N_EXPERTS
    nt = tokens // LANES
    aff_t = aff.T.reshape(N_EXPERTS, nt, LANES)
    blk = pl.BlockSpec((SEL_EXPERTS, nt, LANES), lambda e: (e, 0, 0))
    idx, lp, win = pl.pallas_call(
        functools.partial(_select_kernel, cap=cap),
        grid=(N_EXPERTS // SEL_EXPERTS,),
        in_specs=[blk],
        out_specs=[pl.BlockSpec((SEL_EXPERTS, 1, cap), lambda e: (e, 0, 0)), blk, blk],
        out_shape=[
            jax.ShapeDtypeStruct((N_EXPERTS, 1, cap), I32),
            jax.ShapeDtypeStruct((N_EXPERTS, nt, LANES), F32),
            jax.ShapeDtypeStruct((N_EXPERTS, nt, LANES), F32),
        ],
        compiler_params=_cparams(("parallel",)),
        name="select",
    )(aff_t)
    lp_tok = lp.reshape(N_EXPERTS, tokens).T
    win_start = win[:, ::COMB_TILE // LANES, 0].astype(I32).reshape(-1)
    return idx.reshape(N_EXPERTS * cap), lp_tok, win_start, cap


def _ffn_kernel(idx_ref, x_hbm, wg_ref, wu_ref, wd_ref, ye_ref, land_a, land_b, xe_a, xe_b, sem_ref,
                *, cap):
    e = pl.program_id(0)
    c = pl.program_id(1)
    pairs = cap // (2 * FFN_ROWS)
    step = e * pairs + c
    n_steps = N_EXPERTS * pairs
    last = 2 * n_steps - 1
    land = (land_a, land_b)
    stage = (xe_a, xe_b)

    def row_copy(chunk, r, half):
        tok = idx_ref[chunk * FFN_ROWS + r]
        return pltpu.make_async_copy(x_hbm.at[pl.ds(tok, 1), :], land[half].at[pl.ds(r, 1), :],
                                     sem_ref.at[half])

    def start_chunk(chunk, half):
        for r in range(FFN_ROWS):
            row_copy(chunk, r, half).start(priority=r % 2)

    def wait_chunk(chunk, half):
        def body(r, carry):
            row_copy(chunk, r, half).wait()
            return carry
        lax.fori_loop(0, FFN_ROWS, body, 0, unroll=8)

    @pl.when(step == 0)
    def _():
        start_chunk(0, 0)
        start_chunk(1, 1)

    for half in range(2):
        wait_chunk(2 * step + half, half)
        stage[half][...] = land[half][...].astype(BF16)
        start_chunk(jnp.minimum(2 * step + 2 + half, last), half)
        xe = stage[half][...]
        gate = _dot(xe, wg_ref[...])
        up = _dot(xe, wu_ref[...])
        hid = (gate * _sigmoid(gate) * up).astype(BF16)
        ye_ref[half * FFN_ROWS:(half + 1) * FFN_ROWS, :] = _dot(hid, wd_ref[...]).astype(BF16)

    @pl.when(step == n_steps - 1)
    def _():
        wait_chunk(last, 0)
        wait_chunk(last, 1)


def _ffn_call(idx, x1, wg, wu, wd, cap):
    pairs = cap // (2 * FFN_ROWS)
    wspec = lambda d0, d1: pl.BlockSpec((None, d0, d1), lambda e, c, idx: (e, 0, 0))
    grid_spec = pltpu.PrefetchScalarGridSpec(
        num_scalar_prefetch=1,
        grid=(N_EXPERTS, pairs),
        in_specs=[
            pl.BlockSpec(memory_space=pl.ANY),
            wspec(D_MODEL, D_EXPERT), wspec(D_MODEL, D_EXPERT), wspec(D_EXPERT, D_MODEL),
        ],
        out_specs=pl.BlockSpec((None, 2 * FFN_ROWS, D_MODEL), lambda e, c, idx: (e, c, 0)),
        scratch_shapes=[
            pltpu.VMEM((FFN_ROWS, D_MODEL), F32),
            pltpu.VMEM((FFN_ROWS, D_MODEL), F32),
            pltpu.VMEM((FFN_ROWS, D_MODEL), BF16),
            pltpu.VMEM((FFN_ROWS, D_MODEL), BF16),
            pltpu.SemaphoreType.DMA((2,)),
        ],
    )
    return pl.pallas_call(
        functools.partial(_ffn_kernel, cap=cap),
        grid_spec=grid_spec,
        out_shape=jax.ShapeDtypeStruct((N_EXPERTS, cap, D_MODEL), BF16),
        compiler_params=_cparams(("arbitrary", "arbitrary")),
        name="ffn",
    )(idx, x1, wg, wu, wd)


def _combine_kernel(win_ref, x1_ref, p_ref, lp_ref, aff_ref, ye_hbm, wpg_ref, bpg_ref, wple_ref,
                    g_ref, b_ref, o_ref, buf_ref, xbuf_ref, acc_ref, sem_ref, xsem_ref, *, n_tiles):
    i = pl.program_id(0)
    slot = i % 2
    width = N_EXPERTS * COMB_WIN

    def win_copy(tile, e, rnd, dst, sem):
        start = pl.multiple_of(win_ref[e * n_tiles + tile] + rnd * COMB_WIN, BF16_SUBLANES)
        return pltpu.make_async_copy(ye_hbm.at[e, pl.ds(start, COMB_WIN), :],
                                     dst.at[pl.ds(e * COMB_WIN, COMB_WIN), :], sem)

    def start_tile(tile, buf_slot):
        for e in range(N_EXPERTS):
            win_copy(tile, e, 0, buf_ref.at[buf_slot], sem_ref.at[buf_slot]).start()

    @pl.when(i == 0)
    def _():
        start_tile(0, 0)

    @pl.when(i + 1 < n_tiles)
    def _():
        start_tile(i + 1, 1 - slot)

    x1 = x1_ref[...]
    ple = _sigmoid(_dot(x1.astype(BF16), wpg_ref[...]) + bpg_ref[...]) \
        * _dot(p_ref[...].astype(BF16), wple_ref[...])
    acc_ref[...] = DN_ALPHA * x1 + ple

    lp = lp_ref[...]
    rnd_of = jnp.floor(lp * (1.0 / COMB_WIN))
    row_of = lp - rnd_of * COMB_WIN
    er = lax.broadcasted_iota(I32, (N_EXPERTS, width), 0)
    ec = lax.broadcasted_iota(I32, (N_EXPERTS, width), 1) // COMB_WIN
    expand = jnp.where(er == ec, 1.0, 0.0).astype(BF16)
    lane_row = (lax.broadcasted_iota(I32, (COMB_TILE, width), 1) % COMB_WIN).astype(F32)
    hit = _dot(row_of.astype(BF16), expand) == lane_row
    rnd_e = _dot(rnd_of.astype(BF16), expand)
    gate_e = jnp.where(hit, _dot(aff_ref[...].astype(BF16), expand), 0.0)

    def add_round(rnd, rows):
        lhs = jnp.where(rnd_e == rnd, gate_e, 0.0).astype(BF16)
        acc_ref[...] += _dot(lhs, rows)

    for e in range(N_EXPERTS):
        win_copy(i, e, 0, buf_ref.at[slot], sem_ref.at[slot]).wait()
    add_round(0.0, buf_ref[slot])

    n_rounds = jnp.max(rnd_of).astype(I32) + 1
    for rnd in range(1, COMB_ROUNDS):
        @pl.when(rnd < n_rounds)
        def _():
            for e in range(N_EXPERTS):
                win_copy(i, e, rnd, xbuf_ref, xsem_ref.at[0]).start()
            for e in range(N_EXPERTS):
                win_copy(i, e, rnd, xbuf_ref, xsem_ref.at[0]).wait()
            add_round(float(rnd), xbuf_ref[...])

    o_ref[...] = _layer_norm_rows(acc_ref[...], g_ref[...], b_ref[...])


def _combine_call(win_start, x1, p2d, lp_tok, aff, ye, wpg, bpg, wple, ln_g, ln_b):
    tokens = x1.shape[0]
    n_tiles = tokens // COMB_TILE
    row = lambda i, w: (i, 0)
    fixed = lambda i, w: (0, 0)
    vec = pl.BlockSpec((1, D_MODEL), fixed)
    grid_spec = pltpu.PrefetchScalarGridSpec(
        num_scalar_prefetch=1,
        grid=(n_tiles,),
        in_specs=[
            pl.BlockSpec((COMB_TILE, D_MODEL), row),
            pl.BlockSpec((COMB_TILE, PLE_DIM), row),
            pl.BlockSpec((COMB_TILE, N_EXPERTS), row),
            pl.BlockSpec((COMB_TILE, N_EXPERTS), row),
            pl.BlockSpec(memory_space=pl.ANY),
            pl.BlockSpec((D_MODEL, D_MODEL), fixed),
            vec,
            pl.BlockSpec((PLE_DIM, D_MODEL), fixed),
            vec, vec,
        ],
        out_specs=pl.BlockSpec((COMB_TILE, D_MODEL), row),
        scratch_shapes=[
            pltpu.VMEM((2, N_EXPERTS * COMB_WIN, D_MODEL), BF16),
            pltpu.VMEM((N_EXPERTS * COMB_WIN, D_MODEL), BF16),
            pltpu.VMEM((COMB_TILE, D_MODEL), F32),
            pltpu.SemaphoreType.DMA((2,)),
            pltpu.SemaphoreType.DMA((1,)),
        ],
    )
    return pl.pallas_call(
        functools.partial(_combine_kernel, n_tiles=n_tiles),
        grid_spec=grid_spec,
        out_shape=jax.ShapeDtypeStruct((tokens, D_MODEL), F32),
        compiler_params=_cparams(("arbitrary",)),
        name="combine",
    )(win_start, x1, p2d, lp_tok, aff, ye, wpg, bpg, wple, ln_g, ln_b)


def _prep_weights(ln_in_g, ln_in_b, w_in, q_norm_w, k_norm_w, conv_w, conv_b, dt_bias, a_log,
                  d_skip, ssd_norm_w, w_out, ln1_g, ln1_b, w_router, w_e_gate, w_e_up, w_e_down,
                  w_ple_gate, b_ple_gate, w_ple, ln2_g, ln2_b):
    l = 0
    vec = lambda a: a.reshape(1, -1)
    w_in_pad = jnp.zeros((D_MODEL, IN_COLS_PAD), F32).at[:, :IN_COLS].set(w_in[l]).astype(BF16)
    wr = jnp.zeros((D_MODEL, LANES), F32).at[:, :N_EXPERTS].set(w_router[l])
    wr_hi, wr_lo = _split_bf16(wr)
    return dict(
        ln_in_g=vec(ln_in_g), ln_in_b=vec(ln_in_b), w_in=w_in_pad,
        qw=vec(jnp.tile(q_norm_w[l], N_Q_HEADS)), kw=vec(jnp.tile(k_norm_w[l], N_KV_HEADS)),
        conv_w=conv_w[l], conv_b=conv_b[l], dt_bias=dt_bias[l], a_log=a_log[l], d_skip=d_skip[l],
        norm_w=vec(ssd_norm_w[l]), w_out=w_out[l].astype(BF16), ln1_g=vec(ln1_g[l]), ln1_b=vec(ln1_b[l]),
        wr_hi=wr_hi, wr_lo=wr_lo,
        wg=w_e_gate[l].astype(BF16), wu=w_e_up[l].astype(BF16), wd=w_e_down[l].astype(BF16),
        wpg=w_ple_gate[l].astype(BF16), bpg=vec(b_ple_gate[l]), wple=w_ple[l].astype(BF16),
        ln2_g=vec(ln2_g[l]), ln2_b=vec(ln2_b[l]),
    )


def _trunk(x, p, w):
    batch, seq, _ = x.shape
    tokens = batch * seq
    x2d = x.reshape(tokens, D_MODEL)
    q, k, v, z, xc, dt2 = _inproj_call(x2d, seq, w["ln_in_g"], w["ln_in_b"], w["w_in"], w["qw"], w["kw"],
                                       w["conv_w"], w["conv_b"])
    attn = _attn_call(q, k, v, batch, seq)
    y2 = _ssd_call(xc, dt2, batch, seq, w["dt_bias"], w["a_log"], w["d_skip"])
    x1, aff = _mix_call(x2d, w["ln_in_g"], w["ln_in_b"], attn, y2, z, w["norm_w"], w["w_out"],
                        w["ln1_g"], w["ln1_b"], w["wr_hi"], w["wr_lo"])
    idx, lp_tok, win_start, cap = _select_call(aff)
    ye = _ffn_call(idx, x1, w["wg"], w["wu"], w["wd"], cap)
    out = _combine_call(win_start, x1, p[0].reshape(tokens, PLE_DIM), lp_tok, aff, ye,
                        w["wpg"], w["bpg"], w["wple"], w["ln2_g"], w["ln2_b"])
    return out.reshape(batch, seq, D_MODEL)


def kernel(x_prompt, x_sample, p_prompt, p_sample, ln_in_g, ln_in_b, w_in, q_norm_w, k_norm_w, conv_w, conv_b, dt_bias, a_log, d_skip, ssd_norm_w, w_out, ln1_g, ln1_b, w_router, w_e_gate, w_e_up, w_e_down, w_ple_gate, b_ple_gate, w_ple, ln2_g, ln2_b):
    w = _prep_weights(ln_in_g, ln_in_b, w_in, q_norm_w, k_norm_w, conv_w, conv_b, dt_bias, a_log,
                      d_skip, ssd_norm_w, w_out, ln1_g, ln1_b, w_router, w_e_gate, w_e_up, w_e_down,
                      w_ple_gate, b_ple_gate, w_ple, ln2_g, ln2_b)
    return (_trunk(x_prompt, p_prompt, w), _trunk(x_sample, p_sample, w))
```
